```python
import jax, jax.numpy as jnp
from jax import lax
import numpy as np

D_MODEL = 2048
BATCH = 4
SEQ = 2048
DEPTH = 4
DEC_BATCH = 16
DEC_SEQ = 2048
PAST_LEN = 128

GRID_W = 64
HEAD_DIM = 128
N_HEADS = D_MODEL // HEAD_DIM
N_KV_HEADS = N_HEADS // 4
Q_BLOCK = 128
ROPE_BASE = 10000.0
SG_WIDTH = D_MODEL
SG_GROUPS = 8
SG_CHUNK = 128
N_MEM = 256
X_HEADS = 4
X_HEAD_DIM = D_MODEL // X_HEADS
D_FF = 5632
CONV_W = 3
EPS = 1e-6

Q_COLS = N_HEADS * HEAD_DIM
KV_COLS = N_KV_HEADS * HEAD_DIM
N_IN = Q_COLS + 2 * KV_COLS + 2 * SG_WIDTH + 2 * D_MODEL

kernel_name = "hybrid_gqa_sgu_convffn_encoder"


def rms_norm(x, g):
    xf = x.astype(jnp.float32)
    y = xf * lax.rsqrt(jnp.mean(xf * xf, axis=-1, keepdims=True) + EPS)
    return (y * g.astype(jnp.float32)).astype(x.dtype)


def layer_norm(x, g, b):
    xf = x.astype(jnp.float32)
    mu = jnp.mean(xf, axis=-1, keepdims=True)
    xc = xf - mu
    y = xc * lax.rsqrt(jnp.mean(xc * xc, axis=-1, keepdims=True) + EPS)
    return (y * g.astype(jnp.float32) + b.astype(jnp.float32)).astype(x.dtype)


def axial_rope_tables(seq_len, dtype):
    rows = seq_len // GRID_W
    t = jnp.arange(rows * GRID_W)
    row = (t // GRID_W).astype(jnp.float32)
    col = (t % GRID_W).astype(jnp.float32)
    quarter = HEAD_DIM // 4
    inv = ROPE_BASE ** (-jnp.arange(quarter, dtype=jnp.float32) / quarter)
    ang_r = row[:, None] * inv[None, :]
    ang_c = col[:, None] * inv[None, :]
    ang = jnp.concatenate([ang_r, ang_r, ang_c, ang_c], axis=-1)
    return jnp.cos(ang).astype(dtype), jnp.sin(ang).astype(dtype)


def apply_rope(x, cos, sin):
    quarter = HEAD_DIM // 4
    x4 = x.reshape(x.shape[:-1] + (2, 2, quarter))
    rot = jnp.stack([-x4[..., 1, :], x4[..., 0, :]], axis=-2).reshape(x.shape)
    return x * cos[None, :, None, :] + rot * sin[None, :, None, :]


def grid_attention(q, k, v):
    B, S = q.shape[0], q.shape[1]
    nb = S // Q_BLOCK
    grp = N_HEADS // N_KV_HEADS
    qb = q.reshape(B, nb, Q_BLOCK, N_KV_HEADS, grp, HEAD_DIM).transpose(1, 0, 2, 3, 4, 5)
    scale = HEAD_DIM ** -0.5

    def block(qi):
        s = jnp.einsum('bqkgd,bskd->bkgqs', qi, k).astype(jnp.float32) * scale
        p = jax.nn.softmax(s, axis=-1).astype(v.dtype)
        return jnp.einsum('bkgqs,bskd->bqkgd', p, v)

    o = lax.map(block, qb)
    return o.transpose(1, 0, 2, 3, 4, 5).reshape(B, S, N_HEADS * HEAD_DIM)


def spatial_gating(z, ln_g, ln_b, w_s, b_s):
    B, S = z.shape[0], z.shape[1]
    u, v = jnp.split(z, 2, axis=-1)
    v = layer_norm(v, ln_g, ln_b)
    nc = S // SG_CHUNK
    cg = SG_WIDTH // SG_GROUPS
    vc = v.reshape(B, nc, SG_CHUNK, SG_GROUPS, cg)
    mixed = jnp.einsum('gpq,bnqgc->bnpgc', w_s, vc) + b_s.T[:, :, None]
    return u * mixed.reshape(B, S, SG_WIDTH)


def memory_attention(h, mem, w_xq, w_xkv, w_xo):
    B, S = h.shape[0], h.shape[1]
    n_mem = mem.shape[1]
    q = (h @ w_xq).reshape(B, S, X_HEADS, X_HEAD_DIM)
    k, v = jnp.split(mem @ w_xkv, 2, axis=-1)
    k = k.reshape(B, n_mem, X_HEADS, X_HEAD_DIM)
    v = v.reshape(B, n_mem, X_HEADS, X_HEAD_DIM)
    s = jnp.einsum('bshd,bnhd->bhsn', q, k).astype(jnp.float32) * (X_HEAD_DIM ** -0.5)
    p = jax.nn.softmax(s, axis=-1).astype(v.dtype)
    o = jnp.einsum('bhsn,bnhd->bshd', p, v).reshape(B, S, D_MODEL)
    return o @ w_xo


def depthwise_conv(a, w, b):
    S = a.shape[1]
    pad = CONV_W // 2
    ap = jnp.pad(a, ((0, 0), (pad, CONV_W - 1 - pad), (0, 0)))
    out = b
    for j in range(CONV_W):
        out = out + ap[:, j:j + S] * w[j]
    return out


def trunk(x, mem, params):
    (norm_mix_g, w_in, q_norm_g, k_norm_g, w_attn_o, sg_norm_g, sg_norm_b,
     w_spatial, b_spatial, w_sg_o, w_out, norm_x_g, norm_mem_g, w_xq, w_xkv,
     w_xo, norm_ffn_g, w_ffn_up, conv_w, conv_b, w_ffn_down, final_norm_g) = params
    B, S = x.shape[0], x.shape[1]
    cos, sin = axial_rope_tables(S, x.dtype)
    splits = [Q_COLS, Q_COLS + KV_COLS, Q_COLS + 2 * KV_COLS, Q_COLS + 2 * KV_COLS + 2 * SG_WIDTH]
    for l in range(DEPTH):
        h = rms_norm(x, norm_mix_g[l])
        q, k, v, z, gates = jnp.split(h @ w_in[l], splits, axis=-1)
        q = apply_rope(rms_norm(q.reshape(B, S, N_HEADS, HEAD_DIM), q_norm_g[l]), cos, sin)
        k = apply_rope(rms_norm(k.reshape(B, S, N_KV_HEADS, HEAD_DIM), k_norm_g[l]), cos, sin)
        v = v.reshape(B, S, N_KV_HEADS, HEAD_DIM)
        branch_a = grid_attention(q, k, v) @ w_attn_o[l]
        branch_s = spatial_gating(jax.nn.gelu(z), sg_norm_g[l], sg_norm_b[l],
                                  w_spatial[l], b_spatial[l]) @ w_sg_o[l]
        g_a, g_s = jnp.split(jax.nn.sigmoid(gates), 2, axis=-1)
        x = x + (g_a * branch_a + g_s * branch_s) @ w_out[l]
        x = x + memory_attention(rms_norm(x, norm_x_g[l]), rms_norm(mem, norm_mem_g[l]),
                                 w_xq[l], w_xkv[l], w_xo[l])
        h = rms_norm(x, norm_ffn_g[l])
        a, b = jnp.split(h @ w_ffn_up[l], 2, axis=-1)
        a = depthwise_conv(a, conv_w[l], conv_b[l])
        x = x + (jax.nn.gelu(a) * b) @ w_ffn_down[l]
    return rms_norm(x, final_norm_g)


def setup_inputs(seed: int = 0) -> dict:
    key = jax.random.key(seed)
    ks = jax.random.split(key, 32)

    def nrm(k, shape, scale):
        return jax.random.normal(k, shape, dtype=jnp.float32) * scale

    def gain(k, shape):
        return 1.0 + nrm(k, shape, 0.02)

    return {
        "x_prompt": nrm(ks[0], (BATCH, SEQ, D_MODEL), 1.0),
        "x_sample": nrm(ks[1], (DEC_BATCH, DEC_SEQ, D_MODEL), 1.0),
        "mem_prompt": nrm(ks[2], (BATCH, N_MEM, D_MODEL), 1.0),
        "mem_sample": nrm(ks[3], (DEC_BATCH, N_MEM, D_MODEL), 1.0),
        "norm_mix_g": gain(ks[4], (DEPTH, D_MODEL)),
        "w_in": nrm(ks[5], (DEPTH, D_MODEL, N_IN), D_MODEL ** -0.5),
        "q_norm_g": gain(ks[6], (DEPTH, HEAD_DIM)),
        "k_norm_g": gain(ks[7], (DEPTH, HEAD_DIM)),
        "w_attn_o": nrm(ks[8], (DEPTH, Q_COLS, D_MODEL), Q_COLS ** -0.5),
        "sg_norm_g": gain(ks[9], (DEPTH, SG_WIDTH)),
        "sg_norm_b": nrm(ks[10], (DEPTH, SG_WIDTH), 0.02),
        "w_spatial": nrm(ks[11], (DEPTH, SG_GROUPS, SG_CHUNK, SG_CHUNK), 0.5 * SG_CHUNK ** -0.5),
        "b_spatial": 1.0 + nrm(ks[12], (DEPTH, SG_GROUPS, SG_CHUNK), 0.02),
        "w_sg_o": nrm(ks[13], (DEPTH, SG_WIDTH, D_MODEL), SG_WIDTH ** -0.5),
        "w_out": nrm(ks[14], (DEPTH, D_MODEL, D_MODEL), D_MODEL ** -0.5),
        "norm_x_g": gain(ks[15], (DEPTH, D_MODEL)),
        "norm_mem_g": gain(ks[16], (DEPTH, D_MODEL)),
        "w_xq": nrm(ks[17], (DEPTH, D_MODEL, D_MODEL), D_MODEL ** -0.5),
        "w_xkv": nrm(ks[18], (DEPTH, D_MODEL, 2 * D_MODEL), D_MODEL ** -0.5),
        "w_xo": nrm(ks[19], (DEPTH, D_MODEL, D_MODEL), D_MODEL ** -0.5),
        "norm_ffn_g": gain(ks[20], (DEPTH, D_MODEL)),
        "w_ffn_up": nrm(ks[21], (DEPTH, D_MODEL, 2 * D_FF), D_MODEL ** -0.5),
        "conv_w": nrm(ks[22], (DEPTH, CONV_W, D_FF), CONV_W ** -0.5),
        "conv_b": nrm(ks[23], (DEPTH, D_FF), 0.02),
        "w_ffn_down": nrm(ks[24], (DEPTH, D_FF, D_MODEL), D_FF ** -0.5),
        "final_norm_g": gain(ks[25], (D_MODEL,)),
    }


def reference(x_prompt, x_sample, mem_prompt, mem_sample, norm_mix_g, w_in, q_norm_g,
              k_norm_g, w_attn_o, sg_norm_g, sg_norm_b, w_spatial, b_spatial, w_sg_o,
              w_out, norm_x_g, norm_mem_g, w_xq, w_xkv, w_xo, norm_ffn_g, w_ffn_up,
              conv_w, conv_b, w_ffn_down, final_norm_g):
    params = (norm_mix_g, w_in, q_norm_g, k_norm_g, w_attn_o, sg_norm_g, sg_norm_b,
              w_spatial, b_spatial, w_sg_o, w_out, norm_x_g, norm_mem_g, w_xq, w_xkv,
              w_xo, norm_ffn_g, w_ffn_up, conv_w, conv_b, w_ffn_down, final_norm_g)
    y_prompt = trunk(x_prompt, mem_prompt, params)
    y_sample = trunk(x_sample, mem_sample, params)
    return (y_prompt, y_sample)
```

```python
import functools

import jax
import jax.numpy as jnp
from jax import lax
from jax.experimental import pallas as pl
from jax.experimental.pallas import tpu as pltpu

F32 = jnp.float32
BF16 = jnp.bfloat16

D_MODEL = 2048
GRID_W = 64
HEAD_DIM = 128
N_HEADS = D_MODEL // HEAD_DIM
N_KV_HEADS = N_HEADS // 4
GROUP = N_HEADS // N_KV_HEADS
ROPE_BASE = 10000.0
SG_WIDTH = D_MODEL
SG_GROUPS = 8
SG_CHUNK = 128
SG_GROUP_W = SG_WIDTH // SG_GROUPS
X_HEADS = 4
X_HEAD_DIM = D_MODEL // X_HEADS
D_FF = 5632
EPS = 1e-6

Q_COLS = N_HEADS * HEAD_DIM
KV_COLS = N_KV_HEADS * HEAD_DIM
K_OFF = Q_COLS
V_OFF = K_OFF + KV_COLS
U_OFF = V_OFF + KV_COLS
VS_OFF = U_OFF + SG_WIDTH
GA_OFF = VS_OFF + SG_WIDTH
GS_OFF = GA_OFF + D_MODEL
N_IN = GS_OFF + D_MODEL

LANES = 128
SUBLANES = 8
VMEM_LIMIT_BYTES = 56 * 1024 * 1024

IN_TM = 1024
IN_TN = 512
IN_SUB = 256
ATT_TQ = 512
ATT_SUB = 256
SGU_TR = 512
MIX_TM = 256
HALF = 1024
XA_TM = 512
FFN_TM = 512
FFN_TF = 512
MEM_TM_MAX = 1024
MEM_TN = 512
HALO = SUBLANES


def _params(*sem):
    return pltpu.CompilerParams(dimension_semantics=sem, vmem_limit_bytes=VMEM_LIMIT_BYTES)


def _rms(xf, g):
    ms = jnp.mean(xf * xf, axis=-1, keepdims=True)
    return xf * lax.rsqrt(ms + EPS) * g


def _resident(shape, index_map):
    return pl.BlockSpec(shape, index_map, pipeline_mode=pl.Buffered(1))


def _in_proj_kernel(x_ref, g_ref, w_ref, cos_ref, sina_ref, sinb_ref, qg_ref, kg_ref,
                    o_ref, h_ref):
    j = pl.program_id(1)
    tm = x_ref.shape[0]

    @pl.when(j == 0)
    def _():
        for r in range(0, tm, IN_SUB):
            h_ref[r:r + IN_SUB, :] = _rms(x_ref[r:r + IN_SUB, :], g_ref[...]).astype(BF16)

    def norm_rope(y, r, gain_ref):
        for hh in range(IN_TN // HEAD_DIM):
            cs = slice(hh * HEAD_DIM, (hh + 1) * HEAD_DIM)
            yn = _rms(y[:, cs], gain_ref[...])
            rows = slice(r, r + IN_SUB)
            out = (yn * cos_ref[rows, :]
                   + pltpu.roll(yn, HEAD_DIM - HEAD_DIM // 4, 1) * sina_ref[rows, :]
                   + pltpu.roll(yn, HEAD_DIM // 4, 1) * sinb_ref[rows, :])
            o_ref[rows, cs] = out.astype(BF16)

    def run(epilogue):
        for r in range(0, tm, IN_SUB):
            y = jnp.dot(h_ref[r:r + IN_SUB, :], w_ref[...], preferred_element_type=F32)
            epilogue(y, r)

    def store(fn):
        def ep(y, r):
            o_ref[r:r + IN_SUB, :] = fn(y).astype(BF16)
        return ep

    @pl.when(j < K_OFF // IN_TN)
    def _():
        run(lambda y, r: norm_rope(y, r, qg_ref))

    @pl.when(j == K_OFF // IN_TN)
    def _():
        run(lambda y, r: norm_rope(y, r, kg_ref))

    @pl.when(j == V_OFF // IN_TN)
    def _():
        run(store(lambda y: y))

    @pl.when((j >= U_OFF // IN_TN) & (j < GA_OFF // IN_TN))
    def _():
        run(store(lambda y: jax.nn.gelu(y, approximate=True)))

    @pl.when(j >= GA_OFF // IN_TN)
    def _():
        run(store(jax.nn.sigmoid))


def _in_proj(x, g, w_in, layer, cos, sina, sinb, qg, kg, seq):
    t = x.shape[0]
    tiles_per_seq = seq // IN_TM
    rope_spec = pl.BlockSpec((IN_TM, HEAD_DIM), lambda i, j: (i % tiles_per_seq, 0))
    vec = lambda n: pl.BlockSpec((1, n), lambda i, j: (0, 0))
    return pl.pallas_call(
        _in_proj_kernel,
        grid=(t // IN_TM, N_IN // IN_TN),
        in_specs=[
            pl.BlockSpec((IN_TM, D_MODEL), lambda i, j: (i, 0)),
            vec(D_MODEL),
            pl.BlockSpec((None, D_MODEL, IN_TN), lambda i, j: (layer, 0, j)),
            rope_spec, rope_spec, rope_spec,
            vec(HEAD_DIM), vec(HEAD_DIM),
        ],
        out_specs=pl.BlockSpec((IN_TM, IN_TN), lambda i, j: (i, j)),
        out_shape=jax.ShapeDtypeStruct((t, N_IN), BF16),
        scratch_shapes=[pltpu.VMEM((IN_TM, D_MODEL), BF16)],
        compiler_params=_params("parallel", "arbitrary"),
        name="in_proj",
    )(x, g, w_in, cos, sina, sinb, qg, kg)


def _rms_matmul_kernel(x_ref, g_ref, w_ref, o_ref, h_ref):
    @pl.when(pl.program_id(1) == 0)
    def _():
        h_ref[...] = _rms(x_ref[...], g_ref[...]).astype(BF16)

    o_ref[...] = jnp.dot(h_ref[...], w_ref[...], preferred_element_type=F32).astype(BF16)


def _rms_matmul(x, g, w, layer, tm, tn):
    t = x.shape[0]
    n = w.shape[-1]
    return pl.pallas_call(
        _rms_matmul_kernel,
        grid=(t // tm, n // tn),
        in_specs=[
            pl.BlockSpec((tm, D_MODEL), lambda i, j: (i, 0)),
            pl.BlockSpec((1, D_MODEL), lambda i, j: (0, 0)),
            pl.BlockSpec((None, D_MODEL, tn), lambda i, j: (layer, 0, j)),
        ],
        out_specs=pl.BlockSpec((tm, tn), lambda i, j: (i, j)),
        out_shape=jax.ShapeDtypeStruct((t, n), BF16),
        scratch_shapes=[pltpu.VMEM((tm, D_MODEL), BF16)],
        compiler_params=_params("parallel", "arbitrary"),
        name="mem_kv_proj",
    )(x, g, w)


def _attention_kernel(q_ref, k_ref, v_ref, o_ref):
    k = k_ref[...]
    v = v_ref[...]
    for g in range(GROUP):
        cs = slice(g * HEAD_DIM, (g + 1) * HEAD_DIM)
        for r in range(0, ATT_TQ, ATT_SUB):
            rows = slice(r, r + ATT_SUB)
            s = lax.dot_general(q_ref[rows, cs], k, (((1,), (1,)), ((), ())),
                                preferred_element_type=F32)
            p = jnp.exp(s - jnp.max(s, axis=-1, keepdims=True))
            l = jnp.sum(p, axis=-1, keepdims=True)
            o = jnp.dot(p.astype(BF16), v, preferred_element_type=F32)
            o_ref[rows, cs] = (o / l).astype(BF16)


def _attention(y, seq):
    t = y.shape[0]
    n_seq = t // seq
    q_tiles = seq // ATT_TQ
    gw = GROUP * HEAD_DIM
    return pl.pallas_call(
        _attention_kernel,
        grid=(n_seq, N_KV_HEADS, q_tiles),
        in_specs=[
            pl.BlockSpec((ATT_TQ, gw), lambda b, h, qi: (b * q_tiles + qi, h)),
            pl.BlockSpec((seq, HEAD_DIM), lambda b, h, qi: (b, K_OFF // HEAD_DIM + h)),
            pl.BlockSpec((seq, HEAD_DIM), lambda b, h, qi: (b, V_OFF // HEAD_DIM + h)),
        ],
        out_specs=pl.BlockSpec((ATT_TQ, gw), lambda b, h, qi: (b * q_tiles + qi, h)),
        out_shape=jax.ShapeDtypeStruct((t, Q_COLS), BF16),
        compiler_params=_params("parallel", "parallel", "arbitrary"),
        name="attention",
    )(y, y, y)


def _sgu_kernel(u0_ref, u1_ref, v0_ref, v1_ref, lg_ref, lb_ref, ws_ref, bs_ref, o_ref):
    groups_per_half = HALF // SG_GROUP_W
    for c in range(SGU_TR // SG_CHUNK):
        rows = slice(c * SG_CHUNK, (c + 1) * SG_CHUNK)
        v = jnp.concatenate([v0_ref[rows, :], v1_ref[rows, :]], axis=-1).astype(F32)
        xc = v - jnp.mean(v, axis=-1, keepdims=True)
        var = jnp.mean(xc * xc, axis=-1, keepdims=True)
        vn = (xc * lax.rsqrt(var + EPS) * lg_ref[...] + lb_ref[...]).astype(BF16)
        for g in range(SG_GROUPS):
            cs = slice(g * SG_GROUP_W, (g + 1) * SG_GROUP_W)
            mixed = jnp.dot(ws_ref[g], vn[:, cs], preferred_element_type=F32) + bs_ref[:, g:g + 1]
            u_ref = u0_ref if g < groups_per_half else u1_ref
            gl = g % groups_per_half
            u = u_ref[rows, gl * SG_GROUP_W:(gl + 1) * SG_GROUP_W].astype(F32)
            o_ref[rows, cs] = (u * mixed).astype(BF16)


def _sgu(y, lg, lb, ws, bs_t):
    t = y.shape[0]
    half = lambda off: pl.BlockSpec((SGU_TR, HALF), lambda i: (i, off // HALF))
    vec = pl.BlockSpec((1, SG_WIDTH), lambda i: (0, 0))
    return pl.pallas_call(
        _sgu_kernel,
        grid=(t // SGU_TR,),
        in_specs=[
            half(U_OFF), half(U_OFF + HALF), half(VS_OFF), half(VS_OFF + HALF),
            vec, vec,
            pl.BlockSpec((SG_GROUPS, SG_CHUNK, SG_CHUNK), lambda i: (0, 0, 0)),
            pl.BlockSpec((SG_CHUNK, SG_GROUPS), lambda i: (0, 0)),
        ],
        out_specs=pl.BlockSpec((SGU_TR, SG_WIDTH), lambda i: (i, 0)),
        out_shape=jax.ShapeDtypeStruct((t, SG_WIDTH), BF16),
        compiler_params=_params("parallel"),
        name="spatial_gating",
    )(y, y, y, y, lg, lb, ws, bs_t)


def _mix_kernel(a_ref, s_ref, ga0_ref, ga1_ref, gs0_ref, gs1_ref, x_ref,
                wa_ref, ws_ref, wo_ref, o_ref, mix_ref):
    a_in = a_ref[...]
    s_in = s_ref[...]
    for n, (ga_ref, gs_ref) in enumerate(((ga0_ref, gs0_ref), (ga1_ref, gs1_ref))):
        cs = slice(n * HALF, (n + 1) * HALF)
        br_a = jnp.dot(a_in, wa_ref[:, cs], preferred_element_type=F32)
        br_s = jnp.dot(s_in, ws_ref[:, cs], preferred_element_type=F32)
        mix = ga_ref[...].astype(F32) * br_a + gs_ref[...].astype(F32) * br_s
        mix_ref[:, cs] = mix.astype(BF16)
    for n in range(D_MODEL // HALF):
        cs = slice(n * HALF, (n + 1) * HALF)
        o_ref[:, cs] = x_ref[:, cs] + jnp.dot(mix_ref[...], wo_ref[:, cs],
                                              preferred_element_type=F32)


def _mix(att, sg, y, x, wa, ws, wo, layer):
    t = x.shape[0]
    row = lambda w: pl.BlockSpec((MIX_TM, w), lambda i: (i, 0))
    half = lambda off: pl.BlockSpec((MIX_TM, HALF), lambda i: (i, off // HALF))
    weight = _resident((None, D_MODEL, D_MODEL), lambda i: (layer, 0, 0))
    return pl.pallas_call(
        _mix_kernel,
        grid=(t // MIX_TM,),
        in_specs=[
            row(Q_COLS), row(SG_WIDTH),
            half(GA_OFF), half(GA_OFF + HALF), half(GS_OFF), half(GS_OFF + HALF),
            row(D_MODEL),
            weight, weight, weight,
        ],
        out_specs=row(D_MODEL),
        out_shape=jax.ShapeDtypeStruct((t, D_MODEL), F32),
        scratch_shapes=[pltpu.VMEM((MIX_TM, D_MODEL), BF16)],
        compiler_params=_params("parallel"),
        name="branch_mix",
    )(att, sg, y, y, y, y, x, wa, ws, wo)


def _xattn_kernel(x_ref, g_ref, wq_ref, k_ref, v_ref, wo_ref, o_ref, h_ref, att_ref):
    h_ref[...] = _rms(x_ref[...], g_ref[...]).astype(BF16)
    scale = X_HEAD_DIM ** -0.5
    for hd in range(X_HEADS):
        cs = slice(hd * X_HEAD_DIM, (hd + 1) * X_HEAD_DIM)
        q = jnp.dot(h_ref[...], wq_ref[:, cs], preferred_element_type=F32).astype(BF16)
        s = lax.dot_general(q, k_ref[:, cs], (((1,), (1,)), ((), ())),
                            preferred_element_type=F32) * scale
        p = jnp.exp(s - jnp.max(s, axis=-1, keepdims=True))
        l = jnp.sum(p, axis=-1, keepdims=True)
        o = jnp.dot(p.astype(BF16), v_ref[:, cs], preferred_element_type=F32)
        att_ref[:, cs] = (o / l).astype(BF16)
    for n in range(D_MODEL // HALF):
        cs = slice(n * HALF, (n + 1) * HALF)
        o_ref[:, cs] = x_ref[:, cs] + jnp.dot(att_ref[...], wo_ref[:, cs],
                                              preferred_element_type=F32)


def _xattn(x, g, wq, kv, wo, layer, seq):
    t = x.shape[0]
    n_mem = kv.shape[0] // (t // seq)
    tiles_per_seq = seq // XA_TM
    row = pl.BlockSpec((XA_TM, D_MODEL), lambda i: (i, 0))
    weight = _resident((None, D_MODEL, D_MODEL), lambda i: (layer, 0, 0))
    return pl.pallas_call(
        _xattn_kernel,
        grid=(t // XA_TM,),
        in_specs=[
            row,
            pl.BlockSpec((1, D_MODEL), lambda i: (0, 0)),
            weight,
            pl.BlockSpec((n_mem, D_MODEL), lambda i: (i // tiles_per_seq, 0)),
            pl.BlockSpec((n_mem, D_MODEL), lambda i: (i // tiles_per_seq, 1)),
            weight,
        ],
        out_specs=row,
        out_shape=jax.ShapeDtypeStruct((t, D_MODEL), F32),
        scratch_shapes=[pltpu.VMEM((XA_TM, D_MODEL), BF16), pltpu.VMEM((XA_TM, D_MODEL), BF16)],
        compiler_params=_params("parallel"),
        name="memory_attention",
    )(x, g, wq, kv, kv, wo)


def _ffn_kernel(x_ref, xp_ref, xn_ref, g_ref, wa_ref, wb_ref, cw_ref, cb_ref, wd_ref, fg_ref,
                o_ref, h_ref, hh_ref, *, tiles_per_seq, final_norm):
    i = pl.program_id(0)
    f = pl.program_id(1)
    tm = x_ref.shape[0]

    @pl.when(f == 0)
    def _():
        h_ref[...] = _rms(x_ref[...], g_ref[...]).astype(BF16)
        hh_ref[:HALO, :] = _rms(xp_ref[...], g_ref[...]).astype(BF16)
        hh_ref[HALO:, :] = _rms(xn_ref[...], g_ref[...]).astype(BF16)
        o_ref[...] = x_ref[...]

    a = jnp.dot(h_ref[...], wa_ref[...], preferred_element_type=F32)
    a_halo = jnp.dot(hh_ref[...], wa_ref[...], preferred_element_type=F32)
    b = jnp.dot(h_ref[...], wb_ref[...], preferred_element_type=F32)

    pos = i % tiles_per_seq
    prev_row = jnp.where(pos != 0, a_halo[HALO - 1:HALO, :], 0.0)
    next_row = jnp.where(pos != tiles_per_seq - 1, a_halo[HALO:HALO + 1, :], 0.0)
    ridx = lax.broadcasted_iota(jnp.int32, a.shape, 0)
    a_prev = jnp.where(ridx == 0, prev_row, pltpu.roll(a, 1, 0))
    a_next = jnp.where(ridx == tm - 1, next_row, pltpu.roll(a, tm - 1, 0))
    conv = cb_ref[...] + a_prev * cw_ref[0:1, :] + a * cw_ref[1:2, :] + a_next * cw_ref[2:3, :]
    y = (jax.nn.gelu(conv, approximate=True) * b).astype(BF16)
    o_ref[...] += jnp.dot(y, wd_ref[...], preferred_element_type=F32)

    if final_norm:
        @pl.when(f == pl.num_programs(1) - 1)
        def _():
            o_ref[...] = _rms(o_ref[...], fg_ref[...])


def _ffn(x, g, w_up, conv_w, conv_b, w_down, fg, layer, seq, final_norm):
    t = x.shape[0]
    nf = D_FF // FFN_TF
    tiles_per_seq = seq // FFN_TM
    halo_blocks = FFN_TM // HALO
    last_halo = t // HALO - 1
    row = pl.BlockSpec((FFN_TM, D_MODEL), lambda i, f: (i, 0))
    vec = pl.BlockSpec((1, D_MODEL), lambda i, f: (0, 0))
    kern = functools.partial(_ffn_kernel, tiles_per_seq=tiles_per_seq, final_norm=final_norm)
    return pl.pallas_call(
        kern,
        grid=(t // FFN_TM, nf),
        in_specs=[
            row,
            pl.BlockSpec((HALO, D_MODEL), lambda i, f: (jnp.maximum(i * halo_blocks - 1, 0), 0)),
            pl.BlockSpec((HALO, D_MODEL),
                         lambda i, f: (jnp.minimum((i + 1) * halo_blocks, last_halo), 0)),
            vec,
            pl.BlockSpec((None, D_MODEL, FFN_TF), lambda i, f: (layer, 0, f)),
            pl.BlockSpec((None, D_MODEL, FFN_TF), lambda i, f: (layer, 0, nf + f)),
            pl.BlockSpec((3, FFN_TF), lambda i, f: (0, f)),
            pl.BlockSpec((1, FFN_TF), lambda i, f: (0, f)),
            pl.BlockSpec((None, FFN_TF, D_MODEL), lambda i, f: (layer, f, 0)),
            vec,
        ],
        out_specs=row,
        out_shape=jax.ShapeDtypeStruct((t, D_MODEL), F32),
        scratch_shapes=[pltpu.VMEM((FFN_TM, D_MODEL), BF16), pltpu.VMEM((2 * HALO, D_MODEL), BF16)],
        compiler_params=_params("parallel", "arbitrary"),
        name="conv_ffn",
    )(x, x, x, g, w_up, w_up, conv_w, conv_b, w_down, fg)


def _rope_tables(seq):
    t = jnp.arange(seq)
    row = (t // GRID_W).astype(F32)
    col = (t % GRID_W).astype(F32)
    quarter = HEAD_DIM // 4
    inv = ROPE_BASE ** (-jnp.arange(quarter, dtype=F32) / quarter)
    ang_r = row[:, None] * inv[None, :]
    ang_c = col[:, None] * inv[None, :]
    ang = jnp.concatenate([ang_r, ang_r, ang_c, ang_c], axis=-1)
    cos, sin = jnp.cos(ang), jnp.sin(ang)
    first = (jnp.arange(HEAD_DIM) % (2 * quarter)) < quarter
    return cos, jnp.where(first, -sin, 0.0), jnp.where(first, 0.0, sin)


def kernel(x_prompt, x_sample, mem_prompt, mem_sample, norm_mix_g, w_in, q_norm_g, k_norm_g, w_attn_o, sg_norm_g, sg_norm_b, w_spatial, b_spatial, w_sg_o, w_out, norm_x_g, norm_mem_g, w_xq, w_xkv, w_xo, norm_ffn_g, w_ffn_up, conv_w, conv_b, w_ffn_down, final_norm_g):
    depth = w_in.shape[0]
    seq = x_prompt.shape[1]
    assert x_sample.shape[1] == seq and seq % IN_TM == 0 and seq % GRID_W == 0
    assert w_in.shape[1:] == (D_MODEL, N_IN) and w_ffn_up.shape[2] == 2 * D_FF
    n_prompt = x_prompt.shape[0] * seq

    x = jnp.concatenate([x_prompt.reshape(-1, D_MODEL), x_sample.reshape(-1, D_MODEL)], axis=0)
    mem = jnp.concatenate([mem_prompt.reshape(-1, D_MODEL), mem_sample.reshape(-1, D_MODEL)], axis=0)

    cos, sina, sinb = _rope_tables(seq)
    row = lambda v: v.reshape(1, -1).astype(F32)
    bf = lambda w: w.astype(BF16)
    w_in_b, w_attn_o_b, w_sg_o_b, w_out_b = bf(w_in), bf(w_attn_o), bf(w_sg_o), bf(w_out)
    w_xq_b, w_xkv_b, w_xo_b = bf(w_xq), bf(w_xkv), bf(w_xo)
    w_up_b, w_down_b, w_sp_b = bf(w_ffn_up), bf(w_ffn_down), bf(w_spatial)
    q_scale = HEAD_DIM ** -0.5
    fg = row(final_norm_g)
    mem_tm = MEM_TM_MAX
    while mem.shape[0] % mem_tm:
        mem_tm //= 2
    assert mem_tm % SUBLANES == 0

    for l in range(depth):
        y = _in_proj(x, row(norm_mix_g[l]), w_in_b, l, cos, sina, sinb,
                     row(q_norm_g[l]) * q_scale, row(k_norm_g[l]), seq)
        att = _attention(y, seq)
        sg = _sgu(y, row(sg_norm_g[l]), row(sg_norm_b[l]), w_sp_b[l], b_spatial[l].T)
        x = _mix(att, sg, y, x, w_attn_o_b, w_sg_o_b, w_out_b, l)
        kv = _rms_matmul(mem, row(norm_mem_g[l]), w_xkv_b, l, mem_tm, MEM_TN)
        x = _xattn(x, row(norm_x_g[l]), w_xq_b, kv, w_xo_b, l, seq)
        x = _ffn(x, row(norm_ffn_g[l]), w_up_b, conv_w[l], row(conv_b[l]), w_down_b, fg,
                 l, seq, l == depth - 1)

    y_prompt = x[:n_prompt].reshape(x_prompt.shape)
    y_sample = x[n_prompt:].reshape(x_sample.shape)
    return (y_prompt, y_sample)
```

```python
import functools

import jax
import jax.numpy as jnp
from jax import lax
from jax.experimental import pallas as pl
from jax.experimental.pallas import tpu as pltpu

F32 = jnp.float32
BF16 = jnp.bfloat16

D_MODEL = 2048
GRID_W = 64
HEAD_DIM = 128
N_HEADS = D_MODEL // HEAD_DIM
N_KV_HEADS = N_HEADS // 4
GROUP = N_HEADS // N_KV_HEADS
ROPE_BASE = 10000.0
SG_WIDTH = D_MODEL
SG_GROUPS = 8
SG_CHUNK = 128
SG_GROUP_W = SG_WIDTH // SG_GROUPS
X_HEADS = 4
X_HEAD_DIM = D_MODEL // X_HEADS
D_FF = 5632
EPS = 1e-6
LOG2_E = 1.4426950408889634

Q_COLS = N_HEADS * HEAD_DIM
KV_COLS = N_KV_HEADS * HEAD_DIM
K_OFF = Q_COLS
V_OFF = K_OFF + KV_COLS
U_OFF = V_OFF + KV_COLS
VS_OFF = U_OFF + SG_WIDTH
GA_OFF = VS_OFF + SG_WIDTH
GS_OFF = GA_OFF + D_MODEL
N_IN = GS_OFF + D_MODEL

LANES = 128
SUBLANES = 8
VMEM_LIMIT_BYTES = 56 * 1024 * 1024

IN_TM = 1024
IN_TN = 1024
IN_SUB = 256
ATT_TQ = 512
ATT_SUB = 256
SGU_TR = 512
MIX_TM = 256
HALF = 1024
XA_TM = 512
FFN_TM = 512
FFN_TF = 512
MEM_TM_MAX = 1024
MEM_TN = 512
HALO = SUBLANES


def _params(*sem):
    return pltpu.CompilerParams(dimension_semantics=sem, vmem_limit_bytes=VMEM_LIMIT_BYTES)


def _rms(xf, g):
    ms = jnp.mean(xf * xf, axis=-1, keepdims=True)
    return xf * lax.rsqrt(ms + EPS) * g


def _resident(shape, index_map):
    return pl.BlockSpec(shape, index_map, pipeline_mode=pl.Buffered(1))


def _in_proj_kernel(x_ref, g_ref, w_ref, cos_ref, sina_ref, sinb_ref, qg_ref, kg_ref,
                    o_ref, h_ref):
    j = pl.program_id(1)
    tm = x_ref.shape[0]

    @pl.when(j == 0)
    def _():
        for r in range(0, tm, IN_SUB):
            h_ref[r:r + IN_SUB, :] = _rms(x_ref[r:r + IN_SUB, :], g_ref[...]).astype(BF16)

    def norm_rope(y, rows, n_heads, gain_ref):
        for hh in range(n_heads):
            cs = slice(hh * HEAD_DIM, (hh + 1) * HEAD_DIM)
            yn = _rms(y[:, cs], gain_ref[...])
            out = (yn * cos_ref[rows, :]
                   + pltpu.roll(yn, HEAD_DIM - HEAD_DIM // 4, 1) * sina_ref[rows, :]
                   + pltpu.roll(yn, HEAD_DIM // 4, 1) * sinb_ref[rows, :])
            o_ref[rows, cs] = out.astype(BF16)

    def run(epilogue):
        for r in range(0, tm, IN_SUB):
            rows = slice(r, r + IN_SUB)
            y = jnp.dot(h_ref[rows, :], w_ref[...], preferred_element_type=F32)
            epilogue(y, rows)

    def store(fn):
        def ep(y, rows):
            o_ref[rows, :] = fn(y).astype(BF16)
        return ep

    @pl.when(j < K_OFF // IN_TN)
    def _():
        run(lambda y, rows: norm_rope(y, rows, IN_TN // HEAD_DIM, qg_ref))

    @pl.when(j == K_OFF // IN_TN)
    def _():
        def ep(y, rows):
            norm_rope(y, rows, N_KV_HEADS, kg_ref)
            o_ref[rows, KV_COLS:] = y[:, KV_COLS:].astype(BF16)
        run(ep)

    @pl.when((j >= U_OFF // IN_TN) & (j < GA_OFF // IN_TN))
    def _():
        run(store(lambda y: jax.nn.gelu(y, approximate=True)))

    @pl.when(j >= GA_OFF // IN_TN)
    def _():
        run(store(jax.nn.sigmoid))


def _in_proj(x, g, w_in, layer, cos, sina, sinb, qg, kg, seq):
    t = x.shape[0]
    tiles_per_seq = seq // IN_TM
    rope_spec = pl.BlockSpec((IN_TM, HEAD_DIM), lambda i, j: (i % tiles_per_seq, 0))
    vec = lambda n: pl.BlockSpec((1, n), lambda i, j: (0, 0))
    return pl.pallas_call(
        _in_proj_kernel,
        grid=(t // IN_TM, N_IN // IN_TN),
        in_specs=[
            pl.BlockSpec((IN_TM, D_MODEL), lambda i, j: (i, 0)),
            vec(D_MODEL),
            pl.BlockSpec((None, D_MODEL, IN_TN), lambda i, j: (layer, 0, j)),
            rope_spec, rope_spec, rope_spec,
            vec(HEAD_DIM), vec(HEAD_DIM),
        ],
        out_specs=pl.BlockSpec((IN_TM, IN_TN), lambda i, j: (i, j)),
        out_shape=jax.ShapeDtypeStruct((t, N_IN), BF16),
        scratch_shapes=[pltpu.VMEM((IN_TM, D_MODEL), BF16)],
        compiler_params=_params("parallel", "arbitrary"),
        name="in_proj",
    )(x, g, w_in, cos, sina, sinb, qg, kg)


def _rms_matmul_kernel(x_ref, g_ref, w_ref, o_ref, h_ref):
    @pl.when(pl.program_id(1) == 0)
    def _():
        h_ref[...] = _rms(x_ref[...], g_ref[...]).astype(BF16)

    o_ref[...] = jnp.dot(h_ref[...], w_ref[...], preferred_element_type=F32).astype(BF16)


def _rms_matmul(x, g, w, layer, tm, tn):
    t = x.shape[0]
    n = w.shape[-1]
    return pl.pallas_call(
        _rms_matmul_kernel,
        grid=(t // tm, n // tn),
        in_specs=[
            pl.BlockSpec((tm, D_MODEL), lambda i, j: (i, 0)),
            pl.BlockSpec((1, D_MODEL), lambda i, j: (0, 0)),
            pl.BlockSpec((None, D_MODEL, tn), lambda i, j: (layer, 0, j)),
        ],
        out_specs=pl.BlockSpec((tm, tn), lambda i, j: (i, j)),
        out_shape=jax.ShapeDtypeStruct((t, n), BF16),
        scratch_shapes=[pltpu.VMEM((tm, D_MODEL), BF16)],
        compiler_params=_params("parallel", "arbitrary"),
        name="mem_kv_proj",
    )(x, g, w)


def _attention_kernel(q_ref, k_ref, v_ref, o_ref, kt_ref):
    @pl.when(pl.program_id(2) == 0)
    def _():
        kt_ref[...] = k_ref[...].T

    v = v_ref[...]
    blocks = [(g, r) for g in range(GROUP) for r in range(0, ATT_TQ, ATT_SUB)]

    def scores(g, r):
        q = q_ref[r:r + ATT_SUB, g * HEAD_DIM:(g + 1) * HEAD_DIM]
        return jnp.dot(q, kt_ref[...], preferred_element_type=F32)

    s = scores(*blocks[0])
    for n, (g, r) in enumerate(blocks):
        s_next = scores(*blocks[n + 1]) if n + 1 < len(blocks) else None
        p = jnp.exp2(s - jnp.max(s, axis=-1, keepdims=True))
        l = jnp.sum(p, axis=-1, keepdims=True)
        o = jnp.dot(p.astype(BF16), v, preferred_element_type=F32)
        o_ref[r:r + ATT_SUB, g * HEAD_DIM:(g + 1) * HEAD_DIM] = (o / l).astype(BF16)
        s = s_next


def _attention(y, seq):
    t = y.shape[0]
    n_seq = t // seq
    q_tiles = seq // ATT_TQ
    gw = GROUP * HEAD_DIM
    return pl.pallas_call(
        _attention_kernel,
        grid=(n_seq, N_KV_HEADS, q_tiles),
        in_specs=[
            pl.BlockSpec((ATT_TQ, gw), lambda b, h, qi: (b * q_tiles + qi, h)),
            pl.BlockSpec((seq, HEAD_DIM), lambda b, h, qi: (b, K_OFF // HEAD_DIM + h)),
            pl.BlockSpec((seq, HEAD_DIM), lambda b, h, qi: (b, V_OFF // HEAD_DIM + h)),
        ],
        out_specs=pl.BlockSpec((ATT_TQ, gw), lambda b, h, qi: (b * q_tiles + qi, h)),
        out_shape=jax.ShapeDtypeStruct((t, Q_COLS), BF16),
        scratch_shapes=[pltpu.VMEM((HEAD_DIM, seq), BF16)],
        compiler_params=_params("parallel", "parallel", "arbitrary"),
        name="attention",
    )(y, y, y)


def _sgu_kernel(u0_ref, u1_ref, v0_ref, v1_ref, lg_ref, lb_ref, ws_ref, bs_ref, o_ref):
    groups_per_half = HALF // SG_GROUP_W
    for c in range(SGU_TR // SG_CHUNK):
        rows = slice(c * SG_CHUNK, (c + 1) * SG_CHUNK)
        v = jnp.concatenate([v0_ref[rows, :], v1_ref[rows, :]], axis=-1).astype(F32)
        xc = v - jnp.mean(v, axis=-1, keepdims=True)
        var = jnp.mean(xc * xc, axis=-1, keepdims=True)
        vn = (xc * lax.rsqrt(var + EPS) * lg_ref[...] + lb_ref[...]).astype(BF16)
        for g in range(SG_GROUPS):
            cs = slice(g * SG_GROUP_W, (g + 1) * SG_GROUP_W)
            mixed = jnp.dot(ws_ref[g], vn[:, cs], preferred_element_type=F32) + bs_ref[:, g:g + 1]
            u_ref = u0_ref if g < groups_per_half else u1_ref
            gl = g % groups_per_half
            u = u_ref[rows, gl * SG_GROUP_W:(gl + 1) * SG_GROUP_W].astype(F32)
            o_ref[rows, cs] = (u * mixed).astype(BF16)


def _sgu(y, lg, lb, ws, bs_t):
    t = y.shape[0]
    half = lambda off: pl.BlockSpec((SGU_TR, HALF), lambda i: (i, off // HALF))
    vec = pl.BlockSpec((1, SG_WIDTH), lambda i: (0, 0))
    return pl.pallas_call(
        _sgu_kernel,
        grid=(t // SGU_TR,),
        in_specs=[
            half(U_OFF), half(U_OFF + HALF), half(VS_OFF), half(VS_OFF + HALF),
            vec, vec,
            pl.BlockSpec((SG_GROUPS, SG_CHUNK, SG_CHUNK), lambda i: (0, 0, 0)),
            pl.BlockSpec((SG_CHUNK, SG_GROUPS), lambda i: (0, 0)),
        ],
        out_specs=pl.BlockSpec((SGU_TR, SG_WIDTH), lambda i: (i, 0)),
        out_shape=jax.ShapeDtypeStruct((t, SG_WIDTH), BF16),
        compiler_params=_params("parallel"),
        name="spatial_gating",
    )(y, y, y, y, lg, lb, ws, bs_t)


def _mix_kernel(a_ref, s_ref, ga0_ref, ga1_ref, gs0_ref, gs1_ref, x_ref,
                wa_ref, ws_ref, wo_ref, o_ref, mix_ref):
    a_in = a_ref[...]
    s_in = s_ref[...]
    for n, (ga_ref, gs_ref) in enumerate(((ga0_ref, gs0_ref), (ga1_ref, gs1_ref))):
        cs = slice(n * HALF, (n + 1) * HALF)
        br_a = jnp.dot(a_in, wa_ref[:, cs], preferred_element_type=F32)
        br_s = jnp.dot(s_in, ws_ref[:, cs], preferred_element_type=F32)
        mix = ga_ref[...].astype(F32) * br_a + gs_ref[...].astype(F32) * br_s
        mix_ref[:, cs] = mix.astype(BF16)
    for n in range(D_MODEL // HALF):
        cs = slice(n * HALF, (n + 1) * HALF)
        o_ref[:, cs] = x_ref[:, cs] + jnp.dot(mix_ref[...], wo_ref[:, cs],
                                              preferred_element_type=F32)


def _mix(att, sg, y, x, wa, ws, wo, layer):
    t = x.shape[0]
    row = lambda w: pl.BlockSpec((MIX_TM, w), lambda i: (i, 0))
    half = lambda off: pl.BlockSpec((MIX_TM, HALF), lambda i: (i, off // HALF))
    weight = _resident((None, D_MODEL, D_MODEL), lambda i: (layer, 0, 0))
    return pl.pallas_call(
        _mix_kernel,
        grid=(t // MIX_TM,),
        in_specs=[
            row(Q_COLS), row(SG_WIDTH),
            half(GA_OFF), half(GA_OFF + HALF), half(GS_OFF), half(GS_OFF + HALF),
            row(D_MODEL),
            weight, weight, weight,
        ],
        out_specs=row(D_MODEL),
        out_shape=jax.ShapeDtypeStruct((t, D_MODEL), F32),
        scratch_shapes=[pltpu.VMEM((MIX_TM, D_MODEL), BF16)],
        compiler_params=_params("parallel"),
        name="branch_mix",
    )(att, sg, y, y, y, y, x, wa, ws, wo)


def _xattn_kernel(x_ref, g_ref, wq_ref, k_ref, v_ref, wo_ref, o_ref, h_ref, att_ref):
    h_ref[...] = _rms(x_ref[...], g_ref[...]).astype(BF16)
    scale = X_HEAD_DIM ** -0.5
    for hd in range(X_HEADS):
        cs = slice(hd * X_HEAD_DIM, (hd + 1) * X_HEAD_DIM)
        q = jnp.dot(h_ref[...], wq_ref[:, cs], preferred_element_type=F32).astype(BF16)
        s = lax.dot_general(q, k_ref[:, cs], (((1,), (1,)), ((), ())),
                            preferred_element_type=F32) * scale
        p = jnp.exp(s - jnp.max(s, axis=-1, keepdims=True))
        l = jnp.sum(p, axis=-1, keepdims=True)
        o = jnp.dot(p.astype(BF16), v_ref[:, cs], preferred_element_type=F32)
        att_ref[:, cs] = (o / l).astype(BF16)
    for n in range(D_MODEL // HALF):
        cs = slice(n * HALF, (n + 1) * HALF)
        o_ref[:, cs] = x_ref[:, cs] + jnp.dot(att_ref[...], wo_ref[:, cs],
                                              preferred_element_type=F32)


def _xattn(x, g, wq, kv, wo, layer, seq):
    t = x.shape[0]
    n_mem = kv.shape[0] // (t // seq)
    tiles_per_seq = seq // XA_TM
    row = pl.BlockSpec((XA_TM, D_MODEL), lambda i: (i, 0))
    weight = _resident((None, D_MODEL, D_MODEL), lambda i: (layer, 0, 0))
    return pl.pallas_call(
        _xattn_kernel,
        grid=(t // XA_TM,),
        in_specs=[
            row,
            pl.BlockSpec((1, D_MODEL), lambda i: (0, 0)),
            weight,
            pl.BlockSpec((n_mem, D_MODEL), lambda i: (i // tiles_per_seq, 0)),
            pl.BlockSpec((n_mem, D_MODEL), lambda i: (i // tiles_per_seq, 1)),
            weight,
        ],
        out_specs=row,
        out_shape=jax.ShapeDtypeStruct((t, D_MODEL), F32),
        scratch_shapes=[pltpu.VMEM((XA_TM, D_MODEL), BF16), pltpu.VMEM((XA_TM, D_MODEL), BF16)],
        compiler_params=_params("parallel"),
        name="memory_attention",
    )(x, g, wq, kv, kv, wo)


def _ffn_kernel(x_ref, xp_ref, xn_ref, g_ref, wa_ref, wb_ref, cw_ref, cb_ref, wd_ref, fg_ref,
                o_ref, h_ref, hh_ref, *, tiles_per_seq, final_norm):
    i = pl.program_id(0)
    f = pl.program_id(1)
    tm = x_ref.shape[0]

    @pl.when(f == 0)
    def _():
        h_ref[...] = _rms(x_ref[...], g_ref[...]).astype(BF16)
        hh_ref[:HALO, :] = _rms(xp_ref[...], g_ref[...]).astype(BF16)
        hh_ref[HALO:, :] = _rms(xn_ref[...], g_ref[...]).astype(BF16)
        o_ref[...] = x_ref[...]

    a = jnp.dot(h_ref[...], wa_ref[...], preferred_element_type=F32)
    a_halo = jnp.dot(hh_ref[...], wa_ref[...], preferred_element_type=F32)
    b = jnp.dot(h_ref[...], wb_ref[...], preferred_element_type=F32)

    pos = i % tiles_per_seq
    prev_row = jnp.where(pos != 0, a_halo[HALO - 1:HALO, :], 0.0)
    next_row = jnp.where(pos != tiles_per_seq - 1, a_halo[HALO:HALO + 1, :], 0.0)
    ridx = lax.broadcasted_iota(jnp.int32, a.shape, 0)
    a_prev = jnp.where(ridx == 0, prev_row, pltpu.roll(a, 1, 0))
    a_next = jnp.where(ridx == tm - 1, next_row, pltpu.roll(a, tm - 1, 0))
    conv = cb_ref[...] + a_prev * cw_ref[0:1, :] + a * cw_ref[1:2, :] + a_next * cw_ref[2:3, :]
    y = (jax.nn.gelu(conv, approximate=True) * b).astype(BF16)
    o_ref[...] += jnp.dot(y, wd_ref[...], preferred_element_type=F32)

    if final_norm:
        @pl.when(f == pl.num_programs(1) - 1)
        def _():
            o_ref[...] = _rms(o_ref[...], fg_ref[...])


def _ffn(x, g, w_up, conv_w, conv_b, w_down, fg, layer, seq, final_norm):
    t = x.shape[0]
    nf = D_FF // FFN_TF
    tiles_per_seq = seq // FFN_TM
    halo_blocks = FFN_TM // HALO
    last_halo = t // HALO - 1
    row = pl.BlockSpec((FFN_TM, D_MODEL), lambda i, f: (i, 0))
    vec = pl.BlockSpec((1, D_MODEL), lambda i, f: (0, 0))
    kern = functools.partial(_ffn_kernel, tiles_per_seq=tiles_per_seq, final_norm=final_norm)
    return pl.pallas_call(
        kern,
        grid=(t // FFN_TM, nf),
        in_specs=[
            row,
            pl.BlockSpec((HALO, D_MODEL), lambda i, f: (jnp.maximum(i * halo_blocks - 1, 0), 0)),
            pl.BlockSpec((HALO, D_MODEL),
                         lambda i, f: (jnp.minimum((i + 1) * halo_blocks, last_halo), 0)),
            vec,
            pl.BlockSpec((None, D_MODEL, FFN_TF), lambda i, f: (layer, 0, f)),
            pl.BlockSpec((None, D_MODEL, FFN_TF), lambda i, f: (layer, 0, nf + f)),
            pl.BlockSpec((3, FFN_TF), lambda i, f: (0, f)),
            pl.BlockSpec((1, FFN_TF), lambda i, f: (0, f)),
            pl.BlockSpec((None, FFN_TF, D_MODEL), lambda i, f: (layer, f, 0)),
            vec,
        ],
        out_specs=row,
        out_shape=jax.ShapeDtypeStruct((t, D_MODEL), F32),
        scratch_shapes=[pltpu.VMEM((FFN_TM, D_MODEL), BF16), pltpu.VMEM((2 * HALO, D_MODEL), BF16)],
        compiler_params=_params("parallel", "arbitrary"),
        name="conv_ffn",
    )(x, x, x, g, w_up, w_up, conv_w, conv_b, w_down, fg)


def _rope_tables(seq):
    t = jnp.arange(seq)
    row = (t // GRID_W).astype(F32)
    col = (t % GRID_W).astype(F32)
    quarter = HEAD_DIM // 4
    inv = ROPE_BASE ** (-jnp.arange(quarter, dtype=F32) / quarter)
    ang_r = row[:, None] * inv[None, :]
    ang_c = col[:, None] * inv[None, :]
    ang = jnp.concatenate([ang_r, ang_r, ang_c, ang_c], axis=-1)
    cos, sin = jnp.cos(ang), jnp.sin(ang)
    first = (jnp.arange(HEAD_DIM) % (2 * quarter)) < quarter
    return cos, jnp.where(first, -sin, 0.0), jnp.where(first, 0.0, sin)


def kernel(x_prompt, x_sample, mem_prompt, mem_sample, norm_mix_g, w_in, q_norm_g, k_norm_g, w_attn_o, sg_norm_g, sg_norm_b, w_spatial, b_spatial, w_sg_o, w_out, norm_x_g, norm_mem_g, w_xq, w_xkv, w_xo, norm_ffn_g, w_ffn_up, conv_w, conv_b, w_ffn_down, final_norm_g):
    depth = w_in.shape[0]
    seq = x_prompt.shape[1]
    assert x_sample.shape[1] == seq and seq % IN_TM == 0 and seq % GRID_W == 0
    assert w_in.shape[1:] == (D_MODEL, N_IN) and w_ffn_up.shape[2] == 2 * D_FF
    n_prompt = x_prompt.shape[0] * seq

    x = jnp.concatenate([x_prompt.reshape(-1, D_MODEL), x_sample.reshape(-1, D_MODEL)], axis=0)
    mem = jnp.concatenate([mem_prompt.reshape(-1, D_MODEL), mem_sample.reshape(-1, D_MODEL)], axis=0)

    cos, sina, sinb = _rope_tables(seq)
    row = lambda v: v.reshape(1, -1).astype(F32)
    bf = lambda w: w.astype(BF16)
    w_in_b, w_attn_o_b, w_sg_o_b, w_out_b = bf(w_in), bf(w_attn_o), bf(w_sg_o), bf(w_out)
    w_xq_b, w_xkv_b, w_xo_b = bf(w_xq), bf(w_xkv), bf(w_xo)
    w_up_b, w_down_b, w_sp_b = bf(w_ffn_up), bf(w_ffn_down), bf(w_spatial)
    q_scale = HEAD_DIM ** -0.5 * LOG2_E
    fg = row(final_norm_g)
    mem_tm = MEM_TM_MAX
    while mem.shape[0] % mem_tm:
        mem_tm //= 2
    assert mem_tm % SUBLANES == 0

    for l in range(depth):
        y = _in_proj(x, row(norm_mix_g[l]), w_in_b, l, cos, sina, sinb,
                     row(q_norm_g[l]) * q_scale, row(k_norm_g[l]), seq)
        att = _attention(y, seq)
        sg = _sgu(y, row(sg_norm_g[l]), row(sg_norm_b[l]), w_sp_b[l], b_spatial[l].T)
        x = _mix(att, sg, y, x, w_attn_o_b, w_sg_o_b, w_out_b, l)
        kv = _rms_matmul(mem, row(norm_mem_g[l]), w_xkv_b, l, mem_tm, MEM_TN)
        x = _xattn(x, row(norm_x_g[l]), w_xq_b, kv, w_xo_b, l, seq)
        x = _ffn(x, row(norm_ffn_g[l]), w_up_b, conv_w[l], row(conv_b[l]), w_down_b, fg,
                 l, seq, l == depth - 1)

    y_prompt = x[:n_prompt].reshape(x_prompt.shape)
    y_sample = x[n_prompt:].reshape(x_sample.shape)
    return (y_prompt, y_sample)
```

```python
import functools

import jax
import jax.numpy as jnp
from jax import lax
from jax.experimental import pallas as pl
from jax.experimental.pallas import tpu as pltpu

F32 = jnp.float32
BF16 = jnp.bfloat16

D_MODEL = 2048
GRID_W = 64
HEAD_DIM = 128
N_HEADS = D_MODEL // HEAD_DIM
N_KV_HEADS = N_HEADS // 4
GROUP = N_HEADS // N_KV_HEADS
ROPE_BASE = 10000.0
SG_WIDTH = D_MODEL
SG_GROUPS = 8
SG_CHUNK = 128
SG_GROUP_W = SG_WIDTH // SG_GROUPS
X_HEADS = 4
X_HEAD_DIM = D_MODEL // X_HEADS
D_FF = 5632
EPS = 1e-6
LOG2_E = 1.4426950408889634
GELU_C = 0.7978845608028654

Q_COLS = N_HEADS * HEAD_DIM
KV_COLS = N_KV_HEADS * HEAD_DIM
K_OFF = Q_COLS
V_OFF = K_OFF + KV_COLS
U_OFF = V_OFF + KV_COLS
VS_OFF = U_OFF + SG_WIDTH
GA_OFF = VS_OFF + SG_WIDTH
GS_OFF = GA_OFF + D_MODEL
N_IN = GS_OFF + D_MODEL

LANES = 128
SUBLANES = 8
VMEM_LIMIT_BYTES = 56 * 1024 * 1024

IN_TM = 1024
IN_TN = 1024
IN_SUB = 256
ATT_TQ = 1024
ATT_SUB = 256
SGU_TR = 512
MIX_TM = 256
HALF = 1024
XA_TM = 512
FFN_TM = 512
FFN_TF = 512
MEM_TM_MAX = 1024
MEM_TN = 512
HALO = 2 * SUBLANES


def _params(*sem):
    return pltpu.CompilerParams(dimension_semantics=sem, vmem_limit_bytes=VMEM_LIMIT_BYTES)


def _rms(xf, g):
    ms = jnp.mean(xf * xf, axis=-1, keepdims=True)
    return xf * lax.rsqrt(ms + EPS) * g


def _gelu_tanh(x):
    inner = x * (GELU_C + (GELU_C * 0.044715) * (x * x))
    return (0.5 * x) * (1.0 + jnp.tanh(inner))


def _split_rows(tm, first_rows):
    assert first_rows % tm == 0
    nf = first_rows // tm
    first = pl.BlockSpec((tm, D_MODEL), lambda i, *_: (jnp.minimum(i, nf - 1), 0))
    second = pl.BlockSpec((tm, D_MODEL), lambda i, *_: (jnp.maximum(i - nf, 0), 0))
    return first, second


def _resident(shape, index_map):
    return pl.BlockSpec(shape, index_map, pipeline_mode=pl.Buffered(1))


def _in_proj_kernel(x_ref, g_ref, w_ref, cos_ref, sina_ref, sinb_ref, qg_ref, kg_ref, *refs):
    o_ref, h_ref = refs[-2:]
    j = pl.program_id(1)
    tm = x_ref.shape[0]

    @pl.when(j == 0)
    def _():
        for r in range(0, tm, IN_SUB):
            h_ref[r:r + IN_SUB, :] = _rms(x_ref[r:r + IN_SUB, :], g_ref[...]).astype(BF16)

    def norm_rope(y, rows, n_heads, gain_ref):
        for hh in range(n_heads):
            cs = slice(hh * HEAD_DIM, (hh + 1) * HEAD_DIM)
            yn = _rms(y[:, cs], gain_ref[...])
            out = (yn * cos_ref[rows, :]
                   + pltpu.roll(yn, HEAD_DIM - HEAD_DIM // 4, 1) * sina_ref[rows, :]
                   + pltpu.roll(yn, HEAD_DIM // 4, 1) * sinb_ref[rows, :])
            o_ref[rows, cs] = out.astype(BF16)

    def run(epilogue):
        for r in range(0, tm, IN_SUB):
            rows = slice(r, r + IN_SUB)
            y = jnp.dot(h_ref[rows, :], w_ref[...], preferred_element_type=F32)
            epilogue(y, rows)

    def store(fn):
        def ep(y, rows):
            o_ref[rows, :] = fn(y).astype(BF16)
        return ep

    @pl.when(j < K_OFF // IN_TN)
    def _():
        run(lambda y, rows: norm_rope(y, rows, IN_TN // HEAD_DIM, qg_ref))

    @pl.when(j == K_OFF // IN_TN)
    def _():
        def ep(y, rows):
            norm_rope(y, rows, N_KV_HEADS, kg_ref)
            o_ref[rows, KV_COLS:] = y[:, KV_COLS:].astype(BF16)
        run(ep)

    @pl.when((j >= U_OFF // IN_TN) & (j < GA_OFF // IN_TN))
    def _():
        run(store(_gelu_tanh))

    @pl.when(j >= GA_OFF // IN_TN)
    def _():
        run(store(jax.nn.sigmoid))


def _in_proj(x, total_rows, row_off, y_partial, g, w_in, layer, cos, sina, sinb, qg, kg, seq):
    tile_off = row_off // IN_TM
    tiles_per_seq = seq // IN_TM
    rope_spec = pl.BlockSpec((IN_TM, HEAD_DIM), lambda i, j: (i % tiles_per_seq, 0))
    vec = lambda n: pl.BlockSpec((1, n), lambda i, j: (0, 0))
    in_specs = [
        pl.BlockSpec((IN_TM, D_MODEL), lambda i, j: (i, 0)),
        vec(D_MODEL),
        pl.BlockSpec((None, D_MODEL, IN_TN), lambda i, j: (layer, 0, j)),
        rope_spec, rope_spec, rope_spec,
        vec(HEAD_DIM), vec(HEAD_DIM),
    ]
    args = [x, g, w_in, cos, sina, sinb, qg, kg]
    aliases = {}
    if y_partial is not None:
        aliases = {len(args): 0}
        in_specs.append(pl.BlockSpec(memory_space=pl.ANY))
        args.append(y_partial)
    return pl.pallas_call(
        _in_proj_kernel,
        grid=(x.shape[0] // IN_TM, N_IN // IN_TN),
        in_specs=in_specs,
        out_specs=pl.BlockSpec((IN_TM, IN_TN), lambda i, j: (tile_off + i, j)),
        out_shape=jax.ShapeDtypeStruct((total_rows, N_IN), BF16),
        scratch_shapes=[pltpu.VMEM((IN_TM, D_MODEL), BF16)],
        input_output_aliases=aliases,
        compiler_params=_params("parallel", "arbitrary"),
        name="in_proj",
    )(*args)


def _rms_matmul_kernel(x_ref, g_ref, w_ref, o_ref, h_ref):
    @pl.when(pl.program_id(1) == 0)
    def _():
        h_ref[...] = _rms(x_ref[...], g_ref[...]).astype(BF16)

    o_ref[...] = jnp.dot(h_ref[...], w_ref[...], preferred_element_type=F32).astype(BF16)


def _rms_matmul(x, g, w, layer, tm, tn):
    t = x.shape[0]
    n = w.shape[-1]
    return pl.pallas_call(
        _rms_matmul_kernel,
        grid=(t // tm, n // tn),
        in_specs=[
            pl.BlockSpec((tm, D_MODEL), lambda i, j: (i, 0)),
            pl.BlockSpec((1, D_MODEL), lambda i, j: (0, 0)),
            pl.BlockSpec((None, D_MODEL, tn), lambda i, j: (layer, 0, j)),
        ],
        out_specs=pl.BlockSpec((tm, tn), lambda i, j: (i, j)),
        out_shape=jax.ShapeDtypeStruct((t, n), BF16),
        scratch_shapes=[pltpu.VMEM((tm, D_MODEL), BF16)],
        compiler_params=_params("parallel", "arbitrary"),
        name="mem_kv_proj",
    )(x, g, w)


def _attention_kernel(q_ref, k_ref, v_ref, o_ref, kt_ref):
    @pl.when(pl.program_id(2) == 0)
    def _():
        kt_ref[...] = k_ref[...].T

    v = v_ref[...]
    blocks = [(g, r) for g in range(GROUP) for r in range(0, ATT_TQ, ATT_SUB)]

    def scores(g, r):
        q = q_ref[r:r + ATT_SUB, g * HEAD_DIM:(g + 1) * HEAD_DIM]
        return jnp.dot(q, kt_ref[...], preferred_element_type=F32)

    s = scores(*blocks[0])
    for n, (g, r) in enumerate(blocks):
        s_next = scores(*blocks[n + 1]) if n + 1 < len(blocks) else None
        p = jnp.exp2(s - jnp.max(s, axis=-1, keepdims=True))
        l = jnp.sum(p, axis=-1, keepdims=True)
        o = jnp.dot(p.astype(BF16), v, preferred_element_type=F32)
        o_ref[r:r + ATT_SUB, g * HEAD_DIM:(g + 1) * HEAD_DIM] = (o / l).astype(BF16)
        s = s_next


def _attention(y, seq):
    t = y.shape[0]
    n_seq = t // seq
    q_tiles = seq // ATT_TQ
    gw = GROUP * HEAD_DIM
    return pl.pallas_call(
        _attention_kernel,
        grid=(n_seq, N_KV_HEADS, q_tiles),
        in_specs=[
            pl.BlockSpec((ATT_TQ, gw), lambda b, h, qi: (b * q_tiles + qi, h)),
            pl.BlockSpec((seq, HEAD_DIM), lambda b, h, qi: (b, K_OFF // HEAD_DIM + h)),
            pl.BlockSpec((seq, HEAD_DIM), lambda b, h, qi: (b, V_OFF // HEAD_DIM + h)),
        ],
        out_specs=pl.BlockSpec((ATT_TQ, gw), lambda b, h, qi: (b * q_tiles + qi, h)),
        out_shape=jax.ShapeDtypeStruct((t, Q_COLS), BF16),
        scratch_shapes=[pltpu.VMEM((HEAD_DIM, seq), BF16)],
        compiler_params=_params("parallel", "parallel", "arbitrary"),
        name="attention",
    )(y, y, y)


def _sgu_kernel(u0_ref, u1_ref, v0_ref, v1_ref, lg_ref, lb_ref, ws_ref, bs_ref, o_ref):
    groups_per_half = HALF // SG_GROUP_W
    for c in range(SGU_TR // SG_CHUNK):
        rows = slice(c * SG_CHUNK, (c + 1) * SG_CHUNK)
        v = jnp.concatenate([v0_ref[rows, :], v1_ref[rows, :]], axis=-1).astype(F32)
        xc = v - jnp.mean(v, axis=-1, keepdims=True)
        var = jnp.mean(xc * xc, axis=-1, keepdims=True)
        vn = (xc * lax.rsqrt(var + EPS) * lg_ref[...] + lb_ref[...]).astype(BF16)
        for g in range(SG_GROUPS):
            cs = slice(g * SG_GROUP_W, (g + 1) * SG_GROUP_W)
            mixed = jnp.dot(ws_ref[g], vn[:, cs], preferred_element_type=F32) + bs_ref[:, g:g + 1]
            u_ref = u0_ref if g < groups_per_half else u1_ref
            gl = g % groups_per_half
            u = u_ref[rows, gl * SG_GROUP_W:(gl + 1) * SG_GROUP_W].astype(F32)
            o_ref[rows, cs] = (u * mixed).astype(BF16)


def _sgu(y, lg, lb, ws, bs_t):
    t = y.shape[0]
    half = lambda off: pl.BlockSpec((SGU_TR, HALF), lambda i: (i, off // HALF))
    vec = pl.BlockSpec((1, SG_WIDTH), lambda i: (0, 0))
    return pl.pallas_call(
        _sgu_kernel,
        grid=(t // SGU_TR,),
        in_specs=[
            half(U_OFF), half(U_OFF + HALF), half(VS_OFF), half(VS_OFF + HALF),
            vec, vec,
            pl.BlockSpec((SG_GROUPS, SG_CHUNK, SG_CHUNK), lambda i: (0, 0, 0)),
            pl.BlockSpec((SG_CHUNK, SG_GROUPS), lambda i: (0, 0)),
        ],
        out_specs=pl.BlockSpec((SGU_TR, SG_WIDTH), lambda i: (i, 0)),
        out_shape=jax.ShapeDtypeStruct((t, SG_WIDTH), BF16),
        compiler_params=_params("parallel"),
        name="spatial_gating",
    )(y, y, y, y, lg, lb, ws, bs_t)


def _mix_kernel(a_ref, s_ref, ga0_ref, ga1_ref, gs0_ref, gs1_ref, xa_ref, xb_ref,
                wa_ref, ws_ref, wo_ref, o_ref, mix_ref, *, first_tiles):
    a_in = a_ref[...]
    s_in = s_ref[...]
    for n, (ga_ref, gs_ref) in enumerate(((ga0_ref, gs0_ref), (ga1_ref, gs1_ref))):
        cs = slice(n * HALF, (n + 1) * HALF)
        br_a = jnp.dot(a_in, wa_ref[:, cs], preferred_element_type=F32)
        br_s = jnp.dot(s_in, ws_ref[:, cs], preferred_element_type=F32)
        mix = ga_ref[...].astype(F32) * br_a + gs_ref[...].astype(F32) * br_s
        mix_ref[:, cs] = mix.astype(BF16)
    for n in range(D_MODEL // HALF):
        cs = slice(n * HALF, (n + 1) * HALF)
        x = jnp.where(pl.program_id(0) < first_tiles, xa_ref[:, cs], xb_ref[:, cs])
        o_ref[:, cs] = x + jnp.dot(mix_ref[...], wo_ref[:, cs], preferred_element_type=F32)


def _mix(att, sg, y, xa, xb, first_rows, wa, ws, wo, layer):
    t = att.shape[0]
    xa_spec, xb_spec = _split_rows(MIX_TM, first_rows)
    row = lambda w: pl.BlockSpec((MIX_TM, w), lambda i: (i, 0))
    half = lambda off: pl.BlockSpec((MIX_TM, HALF), lambda i: (i, off // HALF))
    weight = _resident((None, D_MODEL, D_MODEL), lambda i: (layer, 0, 0))
    return pl.pallas_call(
        functools.partial(_mix_kernel, first_tiles=first_rows // MIX_TM),
        grid=(t // MIX_TM,),
        in_specs=[
            row(Q_COLS), row(SG_WIDTH),
            half(GA_OFF), half(GA_OFF + HALF), half(GS_OFF), half(GS_OFF + HALF),
            xa_spec, xb_spec,
            weight, weight, weight,
        ],
        out_specs=row(D_MODEL),
        out_shape=jax.ShapeDtypeStruct((t, D_MODEL), F32),
        scratch_shapes=[pltpu.VMEM((MIX_TM, D_MODEL), BF16)],
        compiler_params=_params("parallel"),
        name="branch_mix",
    )(att, sg, y, y, y, y, xa, xb, wa, ws, wo)


def _xattn_kernel(x_ref, g_ref, wq_ref, k_ref, v_ref, wo_ref, o_ref, h_ref, att_ref):
    h_ref[...] = _rms(x_ref[...], g_ref[...]).astype(BF16)
    scale = X_HEAD_DIM ** -0.5
    for hd in range(X_HEADS):
        cs = slice(hd * X_HEAD_DIM, (hd + 1) * X_HEAD_DIM)
        q = jnp.dot(h_ref[...], wq_ref[:, cs], preferred_element_type=F32).astype(BF16)
        s = lax.dot_general(q, k_ref[:, cs], (((1,), (1,)), ((), ())),
                            preferred_element_type=F32) * scale
        p = jnp.exp(s - jnp.max(s, axis=-1, keepdims=True))
        l = jnp.sum(p, axis=-1, keepdims=True)
        o = jnp.dot(p.astype(BF16), v_ref[:, cs], preferred_element_type=F32)
        att_ref[:, cs] = (o / l).astype(BF16)
    for n in range(D_MODEL // HALF):
        cs = slice(n * HALF, (n + 1) * HALF)
        o_ref[:, cs] = x_ref[:, cs] + jnp.dot(att_ref[...], wo_ref[:, cs],
                                              preferred_element_type=F32)


def _xattn(x, g, wq, kv, wo, layer, seq):
    t = x.shape[0]
    n_mem = kv.shape[0] // (t // seq)
    tiles_per_seq = seq // XA_TM
    row = pl.BlockSpec((XA_TM, D_MODEL), lambda i: (i, 0))
    weight = _resident((None, D_MODEL, D_MODEL), lambda i: (layer, 0, 0))
    return pl.pallas_call(
        _xattn_kernel,
        grid=(t // XA_TM,),
        in_specs=[
            row,
            pl.BlockSpec((1, D_MODEL), lambda i: (0, 0)),
            weight,
            pl.BlockSpec((n_mem, D_MODEL), lambda i: (i // tiles_per_seq, 0)),
            pl.BlockSpec((n_mem, D_MODEL), lambda i: (i // tiles_per_seq, 1)),
            weight,
        ],
        out_specs=row,
        out_shape=jax.ShapeDtypeStruct((t, D_MODEL), F32),
        scratch_shapes=[pltpu.VMEM((XA_TM, D_MODEL), BF16), pltpu.VMEM((XA_TM, D_MODEL), BF16)],
        compiler_params=_params("parallel"),
        name="memory_attention",
    )(x, g, wq, kv, kv, wo)


def _ffn_kernel(x_ref, xp_ref, xn_ref, g_ref, wa_ref, wb_ref, cw_ref, cb_ref, wd_ref, fg_ref,
                *refs, tiles_per_seq, split_tiles):
    if split_tiles is None:
        o_ref, h_ref = refs
        acc_ref = o_ref
    else:
        oa_ref, ob_ref, h_ref, acc_ref = refs
    i = pl.program_id(0)
    f = pl.program_id(1)
    tm = x_ref.shape[0]
    ext = tm + 2 * HALO

    @pl.when(f == 0)
    def _():
        pos = i % tiles_per_seq
        hp = _rms(xp_ref[...], g_ref[...])
        hn = _rms(xn_ref[...], g_ref[...])
        h_ref[:HALO, :] = jnp.where(pos != 0, hp, 0.0).astype(BF16)
        h_ref[HALO:HALO + tm, :] = _rms(x_ref[...], g_ref[...]).astype(BF16)
        h_ref[HALO + tm:, :] = jnp.where(pos != tiles_per_seq - 1, hn, 0.0).astype(BF16)
        acc_ref[...] = x_ref[...]

    a_ext = jnp.dot(h_ref[...], wa_ref[...], preferred_element_type=F32)
    b = jnp.dot(h_ref[HALO:HALO + tm, :], wb_ref[...], preferred_element_type=F32)
    a_prev = pltpu.roll(a_ext, 1, 0)[HALO:HALO + tm]
    a_next = pltpu.roll(a_ext, ext - 1, 0)[HALO:HALO + tm]
    conv = (cb_ref[...] + a_prev * cw_ref[0:1, :] + a_ext[HALO:HALO + tm] * cw_ref[1:2, :]
            + a_next * cw_ref[2:3, :])
    y = (_gelu_tanh(conv) * b).astype(BF16)
    acc_ref[...] += jnp.dot(y, wd_ref[...], preferred_element_type=F32)

    if split_tiles is not None:
        last = f == pl.num_programs(1) - 1

        @pl.when(last & (i < split_tiles))
        def _():
            oa_ref[...] = _rms(acc_ref[...], fg_ref[...])

        @pl.when(last & (i >= split_tiles))
        def _():
            ob_ref[...] = _rms(acc_ref[...], fg_ref[...])


def _ffn(x, g, w_up, conv_w, conv_b, w_down, fg, layer, seq, split_rows):
    t = x.shape[0]
    nf = D_FF // FFN_TF
    tiles_per_seq = seq // FFN_TM
    halo_blocks = FFN_TM // HALO
    last_halo = t // HALO - 1
    row = pl.BlockSpec((FFN_TM, D_MODEL), lambda i, f: (i, 0))
    vec = pl.BlockSpec((1, D_MODEL), lambda i, f: (0, 0))
    scratch = [pltpu.VMEM((FFN_TM + 2 * HALO, D_MODEL), BF16)]
    if split_rows is None:
        split_tiles = None
        out_specs = row
        out_shape = jax.ShapeDtypeStruct((t, D_MODEL), F32)
    else:
        split_tiles = split_rows // FFN_TM
        out_specs = list(_split_rows(FFN_TM, split_rows))
        out_shape = [jax.ShapeDtypeStruct((split_rows, D_MODEL), F32),
                     jax.ShapeDtypeStruct((t - split_rows, D_MODEL), F32)]
        scratch.append(pltpu.VMEM((FFN_TM, D_MODEL), F32))
    kern = functools.partial(_ffn_kernel, tiles_per_seq=tiles_per_seq, split_tiles=split_tiles)
    return pl.pallas_call(
        kern,
        grid=(t // FFN_TM, nf),
        in_specs=[
            row,
            pl.BlockSpec((HALO, D_MODEL), lambda i, f: (jnp.maximum(i * halo_blocks - 1, 0), 0)),
            pl.BlockSpec((HALO, D_MODEL),
                         lambda i, f: (jnp.minimum((i + 1) * halo_blocks, last_halo), 0)),
            vec,
            pl.BlockSpec((None, D_MODEL, FFN_TF), lambda i, f: (layer, 0, f)),
            pl.BlockSpec((None, D_MODEL, FFN_TF), lambda i, f: (layer, 0, nf + f)),
            pl.BlockSpec((3, FFN_TF), lambda i, f: (0, f)),
            pl.BlockSpec((1, FFN_TF), lambda i, f: (0, f)),
            pl.BlockSpec((None, FFN_TF, D_MODEL), lambda i, f: (layer, f, 0)),
            vec,
        ],
        out_specs=out_specs,
        out_shape=out_shape,
        scratch_shapes=scratch,
        compiler_params=_params("parallel" if split_rows is None else "arbitrary", "arbitrary"),
        name="conv_ffn",
    )(x, x, x, g, w_up, w_up, conv_w, conv_b, w_down, fg)


def _rope_tables(seq):
    t = jnp.arange(seq)
    row = (t // GRID_W).astype(F32)
    col = (t % GRID_W).astype(F32)
    quarter = HEAD_DIM // 4
    inv = ROPE_BASE ** (-jnp.arange(quarter, dtype=F32) / quarter)
    ang_r = row[:, None] * inv[None, :]
    ang_c = col[:, None] * inv[None, :]
    ang = jnp.concatenate([ang_r, ang_r, ang_c, ang_c], axis=-1)
    cos, sin = jnp.cos(ang), jnp.sin(ang)
    first = (jnp.arange(HEAD_DIM) % (2 * quarter)) < quarter
    return cos, jnp.where(first, -sin, 0.0), jnp.where(first, 0.0, sin)


def kernel(x_prompt, x_sample, mem_prompt, mem_sample, norm_mix_g, w_in, q_norm_g, k_norm_g, w_attn_o, sg_norm_g, sg_norm_b, w_spatial, b_spatial, w_sg_o, w_out, norm_x_g, norm_mem_g, w_xq, w_xkv, w_xo, norm_ffn_g, w_ffn_up, conv_w, conv_b, w_ffn_down, final_norm_g):
    depth = w_in.shape[0]
    seq = x_prompt.shape[1]
    assert x_sample.shape[1] == seq and seq % IN_TM == 0 and seq % GRID_W == 0
    assert w_in.shape[1:] == (D_MODEL, N_IN) and w_ffn_up.shape[2] == 2 * D_FF
    n_prompt = x_prompt.shape[0] * seq

    xa, xb, first_rows = x_prompt.reshape(-1, D_MODEL), x_sample.reshape(-1, D_MODEL), n_prompt
    total_rows = n_prompt + x_sample.shape[0] * seq
    mem = jnp.concatenate([mem_prompt.reshape(-1, D_MODEL), mem_sample.reshape(-1, D_MODEL)], axis=0)

    cos, sina, sinb = _rope_tables(seq)
    row = lambda v: v.reshape(1, -1).astype(F32)
    bf = lambda w: w.astype(BF16)
    w_in_b, w_attn_o_b, w_sg_o_b, w_out_b = bf(w_in), bf(w_attn_o), bf(w_sg_o), bf(w_out)
    w_xq_b, w_xkv_b, w_xo_b = bf(w_xq), bf(w_xkv), bf(w_xo)
    w_up_b, w_down_b, w_sp_b = bf(w_ffn_up), bf(w_ffn_down), bf(w_spatial)
    q_scale = HEAD_DIM ** -0.5 * LOG2_E
    fg = row(final_norm_g)
    mem_tm = MEM_TM_MAX
    while mem.shape[0] % mem_tm:
        mem_tm //= 2
    assert mem_tm % SUBLANES == 0

    for l in range(depth):
        y = None
        for part, off in ((xa, 0), (xb, first_rows)) if xb is not xa else ((xa, 0),):
            y = _in_proj(part, total_rows, off, y, row(norm_mix_g[l]), w_in_b, l, cos, sina, sinb,
                         row(q_norm_g[l]) * q_scale, row(k_norm_g[l]), seq)
        att = _attention(y, seq)
        sg = _sgu(y, row(sg_norm_g[l]), row(sg_norm_b[l]), w_sp_b[l], b_spatial[l].T)
        x = _mix(att, sg, y, xa, xb, first_rows, w_attn_o_b, w_sg_o_b, w_out_b, l)
        kv = _rms_matmul(mem, row(norm_mem_g[l]), w_xkv_b, l, mem_tm, MEM_TN)
        x = _xattn(x, row(norm_x_g[l]), w_xq_b, kv, w_xo_b, l, seq)
        x = _ffn(x, row(norm_ffn_g[l]), w_up_b, conv_w[l], row(conv_b[l]), w_down_b, fg,
                 l, seq, n_prompt if l == depth - 1 else None)
        xa, xb, first_rows = x, x, total_rows

    y_prompt, y_sample = x
    return (y_prompt.reshape(x_prompt.shape), y_sample.reshape(x_sample.shape))
```

```python
import functools

import jax
import jax.numpy as jnp
from jax import lax
from jax.experimental import pallas as pl
from jax.experimental.pallas import tpu as pltpu

F32 = jnp.float32
BF16 = jnp.bfloat16

D_MODEL = 2048
GRID_W = 64
HEAD_DIM = 128
N_HEADS = D_MODEL // HEAD_DIM
N_KV_HEADS = N_HEADS // 4
GROUP = N_HEADS // N_KV_HEADS
ROPE_BASE = 10000.0
SG_WIDTH = D_MODEL
SG_GROUPS = 8
SG_CHUNK = 128
SG_GROUP_W = SG_WIDTH // SG_GROUPS
X_HEADS = 4
X_HEAD_DIM = D_MODEL // X_HEADS
D_FF = 5632
EPS = 1e-6
LOG2_E = 1.4426950408889634
GELU_C = 0.7978845608028654

Q_COLS = N_HEADS * HEAD_DIM
KV_COLS = N_KV_HEADS * HEAD_DIM
K_OFF = Q_COLS
V_OFF = K_OFF + KV_COLS
U_OFF = V_OFF + KV_COLS
VS_OFF = U_OFF + SG_WIDTH
GA_OFF = VS_OFF + SG_WIDTH
GS_OFF = GA_OFF + D_MODEL
N_IN = GS_OFF + D_MODEL

LANES = 128
SUBLANES = 8
VMEM_LIMIT_BYTES = 56 * 1024 * 1024

IN_TM = 1024
IN_TN = 1024
IN_SUB = 256
ATT_TQ = 1024
ATT_SUB = 256
SGU_TR = 512
MIX_TM = 256
HALF = 1024
XA_TM = 512
FFN_TM = 512
FFN_TF = 512
MEM_TM_MAX = 1024
MEM_TN = 512
HALO = 2 * SUBLANES


def _params(*sem):
    return pltpu.CompilerParams(dimension_semantics=sem, vmem_limit_bytes=VMEM_LIMIT_BYTES)


def _rms(xf, g):
    ms = jnp.mean(xf * xf, axis=-1, keepdims=True)
    return xf * lax.rsqrt(ms + EPS) * g


def _gelu_tanh(x):
    inner = x * (GELU_C + (GELU_C * 0.044715) * (x * x))
    return (0.5 * x) * (1.0 + jnp.tanh(inner))


def _split_rows(tm, first_rows):
    assert first_rows % tm == 0
    nf = first_rows // tm
    first = pl.BlockSpec((tm, D_MODEL), lambda i, *_: (jnp.minimum(i, nf - 1), 0))
    second = pl.BlockSpec((tm, D_MODEL), lambda i, *_: (jnp.maximum(i - nf, 0), 0))
    return first, second


def _col_blocks(w, width):
    *lead, k, n = w.shape
    nl = len(lead)
    return w.reshape(*lead, k, n // width, width).transpose(*range(nl), nl + 1, nl, nl + 2)


def _resident(shape, index_map):
    return pl.BlockSpec(shape, index_map, pipeline_mode=pl.Buffered(1))


def _in_proj_kernel(x_ref, g_ref, w_ref, cos_ref, sina_ref, sinb_ref, qg_ref, kg_ref, *refs):
    o_ref, h_ref = refs[-2:]
    j = pl.program_id(1)
    tm = x_ref.shape[0]

    @pl.when(j == 0)
    def _():
        for r in range(0, tm, IN_SUB):
            h_ref[r:r + IN_SUB, :] = _rms(x_ref[r:r + IN_SUB, :], g_ref[...]).astype(BF16)

    def norm_rope(y, rows, n_heads, gain_ref):
        for hh in range(n_heads):
            cs = slice(hh * HEAD_DIM, (hh + 1) * HEAD_DIM)
            yn = _rms(y[:, cs], gain_ref[...])
            out = (yn * cos_ref[rows, :]
                   + pltpu.roll(yn, HEAD_DIM - HEAD_DIM // 4, 1) * sina_ref[rows, :]
                   + pltpu.roll(yn, HEAD_DIM // 4, 1) * sinb_ref[rows, :])
            o_ref[rows, cs] = out.astype(BF16)

    def run(epilogue):
        for r in range(0, tm, IN_SUB):
            rows = slice(r, r + IN_SUB)
            y = jnp.dot(h_ref[rows, :], w_ref[...], preferred_element_type=F32)
            epilogue(y, rows)

    def store(fn):
        def ep(y, rows):
            o_ref[rows, :] = fn(y).astype(BF16)
        return ep

    @pl.when(j < K_OFF // IN_TN)
    def _():
        run(lambda y, rows: norm_rope(y, rows, IN_TN // HEAD_DIM, qg_ref))

    @pl.when(j == K_OFF // IN_TN)
    def _():
        def ep(y, rows):
            norm_rope(y, rows, N_KV_HEADS, kg_ref)
            o_ref[rows, KV_COLS:] = y[:, KV_COLS:].astype(BF16)
        run(ep)

    @pl.when((j >= U_OFF // IN_TN) & (j < GA_OFF // IN_TN))
    def _():
        run(store(_gelu_tanh))

    @pl.when(j >= GA_OFF // IN_TN)
    def _():
        run(store(jax.nn.sigmoid))


def _in_proj(x, total_rows, row_off, y_partial, g, w_in, layer, cos, sina, sinb, qg, kg, seq):
    assert IN_TN == HALF
    tile_off = row_off // IN_TM
    tiles_per_seq = seq // IN_TM
    rope_spec = pl.BlockSpec((IN_TM, HEAD_DIM), lambda i, j: (i % tiles_per_seq, 0))
    vec = lambda n: pl.BlockSpec((1, n), lambda i, j: (0, 0))
    in_specs = [
        pl.BlockSpec((IN_TM, D_MODEL), lambda i, j: (i, 0)),
        vec(D_MODEL),
        pl.BlockSpec((None, None, D_MODEL, IN_TN), lambda i, j: (layer, j, 0, 0)),
        rope_spec, rope_spec, rope_spec,
        vec(HEAD_DIM), vec(HEAD_DIM),
    ]
    args = [x, g, w_in, cos, sina, sinb, qg, kg]
    aliases = {}
    if y_partial is not None:
        aliases = {len(args): 0}
        in_specs.append(pl.BlockSpec(memory_space=pl.ANY))
        args.append(y_partial)
    return pl.pallas_call(
        _in_proj_kernel,
        grid=(x.shape[0] // IN_TM, N_IN // IN_TN),
        in_specs=in_specs,
        out_specs=pl.BlockSpec((None, IN_TM, IN_TN), lambda i, j: (j, tile_off + i, 0)),
        out_shape=jax.ShapeDtypeStruct((N_IN // IN_TN, total_rows, IN_TN), BF16),
        scratch_shapes=[pltpu.VMEM((IN_TM, D_MODEL), BF16)],
        input_output_aliases=aliases,
        compiler_params=_params("parallel", "arbitrary"),
        name="in_proj",
    )(*args)


def _rms_matmul_kernel(x_ref, g_ref, w_ref, o_ref, h_ref):
    @pl.when(pl.program_id(1) == 0)
    def _():
        h_ref[...] = _rms(x_ref[...], g_ref[...]).astype(BF16)

    o_ref[...] = jnp.dot(h_ref[...], w_ref[...], preferred_element_type=F32).astype(BF16)


def _rms_matmul(x, g, w, layer, tm, tn):
    t = x.shape[0]
    n = w.shape[1] * tn
    return pl.pallas_call(
        _rms_matmul_kernel,
        grid=(t // tm, n // tn),
        in_specs=[
            pl.BlockSpec((tm, D_MODEL), lambda i, j: (i, 0)),
            pl.BlockSpec((1, D_MODEL), lambda i, j: (0, 0)),
            pl.BlockSpec((None, None, D_MODEL, tn), lambda i, j: (layer, j, 0, 0)),
        ],
        out_specs=pl.BlockSpec((tm, tn), lambda i, j: (i, j)),
        out_shape=jax.ShapeDtypeStruct((t, n), BF16),
        scratch_shapes=[pltpu.VMEM((tm, D_MODEL), BF16)],
        compiler_params=_params("parallel", "arbitrary"),
        name="mem_kv_proj",
    )(x, g, w)


def _attention_kernel(q_ref, k_ref, v_ref, o_ref, kt_ref):
    @pl.when(pl.program_id(2) == 0)
    def _():
        kt_ref[...] = k_ref[...].T

    v = v_ref[...]
    blocks = [(g, r) for g in range(GROUP) for r in range(0, ATT_TQ, ATT_SUB)]

    def scores(g, r):
        q = q_ref[r:r + ATT_SUB, g * HEAD_DIM:(g + 1) * HEAD_DIM]
        return jnp.dot(q, kt_ref[...], preferred_element_type=F32)

    s = scores(*blocks[0])
    for n, (g, r) in enumerate(blocks):
        s_next = scores(*blocks[n + 1]) if n + 1 < len(blocks) else None
        p = jnp.exp2(s - jnp.max(s, axis=-1, keepdims=True))
        l = jnp.sum(p, axis=-1, keepdims=True)
        o = jnp.dot(p.astype(BF16), v, preferred_element_type=F32)
        o_ref[r:r + ATT_SUB, g * HEAD_DIM:(g + 1) * HEAD_DIM] = (o / l).astype(BF16)
        s = s_next


def _attention(y, seq):
    t = y.shape[1]
    n_seq = t // seq
    q_tiles = seq // ATT_TQ
    gw = GROUP * HEAD_DIM
    groups_per_half = HALF // gw
    kv_half = K_OFF // HALF
    assert KV_COLS * 2 == HALF
    return pl.pallas_call(
        _attention_kernel,
        grid=(n_seq, N_KV_HEADS, q_tiles),
        in_specs=[
            pl.BlockSpec((None, ATT_TQ, gw), lambda b, h, qi: (h // groups_per_half, b * q_tiles + qi,
                                                               h % groups_per_half)),
            pl.BlockSpec((None, seq, HEAD_DIM), lambda b, h, qi: (kv_half, b, h)),
            pl.BlockSpec((None, seq, HEAD_DIM), lambda b, h, qi: (kv_half, b, N_KV_HEADS + h)),
        ],
        out_specs=pl.BlockSpec((ATT_TQ, gw), lambda b, h, qi: (b * q_tiles + qi, h)),
        out_shape=jax.ShapeDtypeStruct((t, Q_COLS), BF16),
        scratch_shapes=[pltpu.VMEM((HEAD_DIM, seq), BF16)],
        compiler_params=_params("parallel", "parallel", "arbitrary"),
        name="attention",
    )(y, y, y)


def _sgu_kernel(u0_ref, u1_ref, v0_ref, v1_ref, lg_ref, lb_ref, ws_ref, bs_ref, o_ref):
    groups_per_half = HALF // SG_GROUP_W
    for c in range(SGU_TR // SG_CHUNK):
        rows = slice(c * SG_CHUNK, (c + 1) * SG_CHUNK)
        v = jnp.concatenate([v0_ref[rows, :], v1_ref[rows, :]], axis=-1).astype(F32)
        xc = v - jnp.mean(v, axis=-1, keepdims=True)
        var = jnp.mean(xc * xc, axis=-1, keepdims=True)
        vn = (xc * lax.rsqrt(var + EPS) * lg_ref[...] + lb_ref[...]).astype(BF16)
        for g in range(SG_GROUPS):
            cs = slice(g * SG_GROUP_W, (g + 1) * SG_GROUP_W)
            mixed = jnp.dot(ws_ref[g], vn[:, cs], preferred_element_type=F32) + bs_ref[:, g:g + 1]
            u_ref = u0_ref if g < groups_per_half else u1_ref
            gl = g % groups_per_half
            u = u_ref[rows, gl * SG_GROUP_W:(gl + 1) * SG_GROUP_W].astype(F32)
            o_ref[rows, cs] = (u * mixed).astype(BF16)


def _sgu(y, lg, lb, ws, bs_t):
    t = y.shape[1]
    half = lambda off: pl.BlockSpec((None, SGU_TR, HALF), lambda i: (off // HALF, i, 0))
    vec = pl.BlockSpec((1, SG_WIDTH), lambda i: (0, 0))
    return pl.pallas_call(
        _sgu_kernel,
        grid=(t // SGU_TR,),
        in_specs=[
            half(U_OFF), half(U_OFF + HALF), half(VS_OFF), half(VS_OFF + HALF),
            vec, vec,
            pl.BlockSpec((SG_GROUPS, SG_CHUNK, SG_CHUNK), lambda i: (0, 0, 0)),
            pl.BlockSpec((SG_CHUNK, SG_GROUPS), lambda i: (0, 0)),
        ],
        out_specs=pl.BlockSpec((SGU_TR, SG_WIDTH), lambda i: (i, 0)),
        out_shape=jax.ShapeDtypeStruct((t, SG_WIDTH), BF16),
        compiler_params=_params("parallel"),
        name="spatial_gating",
    )(y, y, y, y, lg, lb, ws, bs_t)


def _mix_kernel(a_ref, s_ref, ga0_ref, ga1_ref, gs0_ref, gs1_ref, xa_ref, xb_ref,
                wa_ref, ws_ref, wo_ref, o_ref, mix_ref, *, first_tiles):
    a_in = a_ref[...]
    s_in = s_ref[...]
    for n, (ga_ref, gs_ref) in enumerate(((ga0_ref, gs0_ref), (ga1_ref, gs1_ref))):
        cs = slice(n * HALF, (n + 1) * HALF)
        br_a = jnp.dot(a_in, wa_ref[:, cs], preferred_element_type=F32)
        br_s = jnp.dot(s_in, ws_ref[:, cs], preferred_element_type=F32)
        mix = ga_ref[...].astype(F32) * br_a + gs_ref[...].astype(F32) * br_s
        mix_ref[:, cs] = mix.astype(BF16)
    for n in range(D_MODEL // HALF):
        cs = slice(n * HALF, (n + 1) * HALF)
        x = jnp.where(pl.program_id(0) < first_tiles, xa_ref[:, cs], xb_ref[:, cs])
        o_ref[:, cs] = x + jnp.dot(mix_ref[...], wo_ref[:, cs], preferred_element_type=F32)


def _mix(att, sg, y, xa, xb, first_rows, wa, ws, wo, layer):
    t = att.shape[0]
    xa_spec, xb_spec = _split_rows(MIX_TM, first_rows)
    row = lambda w: pl.BlockSpec((MIX_TM, w), lambda i: (i, 0))
    half = lambda off: pl.BlockSpec((None, MIX_TM, HALF), lambda i: (off // HALF, i, 0))
    weight = _resident((None, D_MODEL, D_MODEL), lambda i: (layer, 0, 0))
    return pl.pallas_call(
        functools.partial(_mix_kernel, first_tiles=first_rows // MIX_TM),
        grid=(t // MIX_TM,),
        in_specs=[
            row(Q_COLS), row(SG_WIDTH),
            half(GA_OFF), half(GA_OFF + HALF), half(GS_OFF), half(GS_OFF + HALF),
            xa_spec, xb_spec,
            weight, weight, weight,
        ],
        out_specs=row(D_MODEL),
        out_shape=jax.ShapeDtypeStruct((t, D_MODEL), F32),
        scratch_shapes=[pltpu.VMEM((MIX_TM, D_MODEL), BF16)],
        compiler_params=_params("parallel"),
        name="branch_mix",
    )(att, sg, y, y, y, y, xa, xb, wa, ws, wo)


def _xattn_kernel(x_ref, g_ref, wq_ref, k_ref, v_ref, wo_ref, o_ref, h_ref, att_ref):
    h_ref[...] = _rms(x_ref[...], g_ref[...]).astype(BF16)
    scale = X_HEAD_DIM ** -0.5
    heads = [slice(hd * X_HEAD_DIM, (hd + 1) * X_HEAD_DIM) for hd in range(X_HEADS)]

    def q_proj(cs):
        return jnp.dot(h_ref[...], wq_ref[:, cs], preferred_element_type=F32).astype(BF16)

    q = q_proj(heads[0])
    for n, cs in enumerate(heads):
        s = lax.dot_general(q, k_ref[:, cs], (((1,), (1,)), ((), ())),
                            preferred_element_type=F32) * scale
        q = q_proj(heads[n + 1]) if n + 1 < len(heads) else None
        p = jnp.exp(s - jnp.max(s, axis=-1, keepdims=True))
        l = jnp.sum(p, axis=-1, keepdims=True)
        o = jnp.dot(p.astype(BF16), v_ref[:, cs], preferred_element_type=F32)
        att_ref[:, cs] = (o / l).astype(BF16)
    for n in range(D_MODEL // HALF):
        cs = slice(n * HALF, (n + 1) * HALF)
        o_ref[:, cs] = x_ref[:, cs] + jnp.dot(att_ref[...], wo_ref[:, cs],
                                              preferred_element_type=F32)


def _xattn(x, g, wq, kv, wo, layer, seq):
    t = x.shape[0]
    n_mem = kv.shape[0] // (t // seq)
    tiles_per_seq = seq // XA_TM
    row = pl.BlockSpec((XA_TM, D_MODEL), lambda i: (i, 0))
    weight = _resident((None, D_MODEL, D_MODEL), lambda i: (layer, 0, 0))
    return pl.pallas_call(
        _xattn_kernel,
        grid=(t // XA_TM,),
        in_specs=[
            row,
            pl.BlockSpec((1, D_MODEL), lambda i: (0, 0)),
            weight,
            pl.BlockSpec((n_mem, D_MODEL), lambda i: (i // tiles_per_seq, 0)),
            pl.BlockSpec((n_mem, D_MODEL), lambda i: (i // tiles_per_seq, 1)),
            weight,
        ],
        out_specs=row,
        out_shape=jax.ShapeDtypeStruct((t, D_MODEL), F32),
        scratch_shapes=[pltpu.VMEM((XA_TM, D_MODEL), BF16), pltpu.VMEM((XA_TM, D_MODEL), BF16)],
        compiler_params=_params("parallel"),
        name="memory_attention",
    )(x, g, wq, kv, kv, wo)


def _ffn_kernel(x_ref, xp_ref, xn_ref, g_ref, wa_ref, wb_ref, cw_ref, cb_ref, wd_ref, fg_ref,
                *refs, tiles_per_seq, split_tiles):
    if split_tiles is None:
        o_ref, h_ref = refs
        acc_ref = o_ref
    else:
        oa_ref, ob_ref, h_ref, acc_ref = refs
    i = pl.program_id(0)
    f = pl.program_id(1)
    tm = x_ref.shape[0]
    ext = tm + 2 * HALO

    @pl.when(f == 0)
    def _():
        pos = i % tiles_per_seq
        hp = _rms(xp_ref[...], g_ref[...])
        hn = _rms(xn_ref[...], g_ref[...])
        h_ref[:HALO, :] = jnp.where(pos != 0, hp, 0.0).astype(BF16)
        h_ref[HALO:HALO + tm, :] = _rms(x_ref[...], g_ref[...]).astype(BF16)
        h_ref[HALO + tm:, :] = jnp.where(pos != tiles_per_seq - 1, hn, 0.0).astype(BF16)
        acc_ref[...] = x_ref[...]

    a_ext = jnp.dot(h_ref[...], wa_ref[...], preferred_element_type=F32)
    b = jnp.dot(h_ref[HALO:HALO + tm, :], wb_ref[...], preferred_element_type=F32)
    a_prev = pltpu.roll(a_ext, 1, 0)[HALO:HALO + tm]
    a_next = pltpu.roll(a_ext, ext - 1, 0)[HALO:HALO + tm]
    conv = (cb_ref[...] + a_prev * cw_ref[0:1, :] + a_ext[HALO:HALO + tm] * cw_ref[1:2, :]
            + a_next * cw_ref[2:3, :])
    y = (_gelu_tanh(conv) * b).astype(BF16)
    acc_ref[...] += jnp.dot(y, wd_ref[...], preferred_element_type=F32)

    if split_tiles is not None:
        last = f == pl.num_programs(1) - 1

        @pl.when(last & (i < split_tiles))
        def _():
            oa_ref[...] = _rms(acc_ref[...], fg_ref[...])

        @pl.when(last & (i >= split_tiles))
        def _():
            ob_ref[...] = _rms(acc_ref[...], fg_ref[...])


def _ffn(x, g, w_up, conv_w, conv_b, w_down, fg, layer, seq, split_rows):
    t = x.shape[0]
    nf = D_FF // FFN_TF
    tiles_per_seq = seq // FFN_TM
    halo_blocks = FFN_TM // HALO
    last_halo = t // HALO - 1
    row = pl.BlockSpec((FFN_TM, D_MODEL), lambda i, f: (i, 0))
    vec = pl.BlockSpec((1, D_MODEL), lambda i, f: (0, 0))
    scratch = [pltpu.VMEM((FFN_TM + 2 * HALO, D_MODEL), BF16)]
    if split_rows is None:
        split_tiles = None
        out_specs = row
        out_shape = jax.ShapeDtypeStruct((t, D_MODEL), F32)
    else:
        split_tiles = split_rows // FFN_TM
        out_specs = list(_split_rows(FFN_TM, split_rows))
        out_shape = [jax.ShapeDtypeStruct((split_rows, D_MODEL), F32),
                     jax.ShapeDtypeStruct((t - split_rows, D_MODEL), F32)]
        scratch.append(pltpu.VMEM((FFN_TM, D_MODEL), F32))
    kern = functools.partial(_ffn_kernel, tiles_per_seq=tiles_per_seq, split_tiles=split_tiles)
    return pl.pallas_call(
        kern,
        grid=(t // FFN_TM, nf),
        in_specs=[
            row,
            pl.BlockSpec((HALO, D_MODEL), lambda i, f: (jnp.maximum(i * halo_blocks - 1, 0), 0)),
            pl.BlockSpec((HALO, D_MODEL),
                         lambda i, f: (jnp.minimum((i + 1) * halo_blocks, last_halo), 0)),
            vec,
            pl.BlockSpec((None, None, D_MODEL, FFN_TF), lambda i, f: (layer, f, 0, 0)),
            pl.BlockSpec((None, None, D_MODEL, FFN_TF), lambda i, f: (layer, nf + f, 0, 0)),
            pl.BlockSpec((3, FFN_TF), lambda i, f: (0, f)),
            pl.BlockSpec((1, FFN_TF), lambda i, f: (0, f)),
            pl.BlockSpec((None, FFN_TF, D_MODEL), lambda i, f: (layer, f, 0)),
            vec,
        ],
        out_specs=out_specs,
        out_shape=out_shape,
        scratch_shapes=scratch,
        compiler_params=_params("parallel" if split_rows is None else "arbitrary", "arbitrary"),
        name="conv_ffn",
    )(x, x, x, g, w_up, w_up, conv_w, conv_b, w_down, fg)


def _rope_tables(seq):
    t = jnp.arange(seq)
    row = (t // GRID_W).astype(F32)
    col = (t % GRID_W).astype(F32)
    quarter = HEAD_DIM // 4
    inv = ROPE_BASE ** (-jnp.arange(quarter, dtype=F32) / quarter)
    ang_r = row[:, None] * inv[None, :]
    ang_c = col[:, None] * inv[None, :]
    ang = jnp.concatenate([ang_r, ang_r, ang_c, ang_c], axis=-1)
    cos, sin = jnp.cos(ang), jnp.sin(ang)
    first = (jnp.arange(HEAD_DIM) % (2 * quarter)) < quarter
    return cos, jnp.where(first, -sin, 0.0), jnp.where(first, 0.0, sin)


def kernel(x_prompt, x_sample, mem_prompt, mem_sample, norm_mix_g, w_in, q_norm_g, k_norm_g, w_attn_o, sg_norm_g, sg_norm_b, w_spatial, b_spatial, w_sg_o, w_out, norm_x_g, norm_mem_g, w_xq, w_xkv, w_xo, norm_ffn_g, w_ffn_up, conv_w, conv_b, w_ffn_down, final_norm_g):
    depth = w_in.shape[0]
    seq = x_prompt.shape[1]
    assert x_sample.shape[1] == seq and seq % IN_TM == 0 and seq % GRID_W == 0
    assert w_in.shape[1:] == (D_MODEL, N_IN) and w_ffn_up.shape[2] == 2 * D_FF
    n_prompt = x_prompt.shape[0] * seq

    xa, xb, first_rows = x_prompt.reshape(-1, D_MODEL), x_sample.reshape(-1, D_MODEL), n_prompt
    total_rows = n_prompt + x_sample.shape[0] * seq
    mem = jnp.concatenate([mem_prompt.reshape(-1, D_MODEL), mem_sample.reshape(-1, D_MODEL)], axis=0)

    cos, sina, sinb = _rope_tables(seq)
    row = lambda v: v.reshape(1, -1).astype(F32)
    bf = lambda w: w.astype(BF16)
    w_in_b, w_xkv_b, w_up_b = _col_blocks(bf(w_in), IN_TN), _col_blocks(bf(w_xkv), MEM_TN), _col_blocks(bf(w_ffn_up), FFN_TF)
    w_attn_o_b, w_sg_o_b, w_out_b, w_xq_b, w_xo_b = bf(w_attn_o), bf(w_sg_o), bf(w_out), bf(w_xq), bf(w_xo)
    w_down_b, w_sp_b = bf(w_ffn_down), bf(w_spatial)
    q_scale = HEAD_DIM ** -0.5 * LOG2_E
    fg = row(final_norm_g)
    mem_tm = MEM_TM_MAX
    while mem.shape[0] % mem_tm:
        mem_tm //= 2
    assert mem_tm % SUBLANES == 0

    for l in range(depth):
        y = None
        for part, off in ((xa, 0), (xb, first_rows)) if xb is not xa else ((xa, 0),):
            y = _in_proj(part, total_rows, off, y, row(norm_mix_g[l]), w_in_b, l, cos, sina, sinb,
                         row(q_norm_g[l]) * q_scale, row(k_norm_g[l]), seq)
        att = _attention(y, seq)
        sg = _sgu(y, row(sg_norm_g[l]), row(sg_norm_b[l]), w_sp_b[l], b_spatial[l].T)
        x = _mix(att, sg, y, xa, xb, first_rows, w_attn_o_b, w_sg_o_b, w_out_b, l)
        kv = _rms_matmul(mem, row(norm_mem_g[l]), w_xkv_b, l, mem_tm, MEM_TN)
        x = _xattn(x, row(norm_x_g[l]), w_xq_b, kv, w_xo_b, l, seq)
        x = _ffn(x, row(norm_ffn_g[l]), w_up_b, conv_w[l], row(conv_b[l]), w_down_b, fg,
                 l, seq, n_prompt if l == depth - 1 else None)
        xa, xb, first_rows = x, x, total_rows

    y_prompt, y_sample = x
    return (y_prompt.reshape(x_prompt.shape), y_sample.reshape(x_sample.shape))
```

```python
import functools

import jax
import jax.numpy as jnp
from jax import lax
from jax.experimental import pallas as pl
from jax.experimental.pallas import tpu as pltpu

F32 = jnp.float32
BF16 = jnp.bfloat16

D_MODEL = 2048
GRID_W = 64
HEAD_DIM = 128
N_HEADS = D_MODEL // HEAD_DIM
N_KV_HEADS = N_HEADS // 4
GROUP = N_HEADS // N_KV_HEADS
ROPE_BASE = 10000.0
SG_WIDTH = D_MODEL
SG_GROUPS = 8
SG_CHUNK = 128
SG_GROUP_W = SG_WIDTH // SG_GROUPS
X_HEADS = 4
X_HEAD_DIM = D_MODEL // X_HEADS
D_FF = 5632
EPS = 1e-6
LOG2_E = 1.4426950408889634
GELU_C = 0.7978845608028654

Q_COLS = N_HEADS * HEAD_DIM
KV_COLS = N_KV_HEADS * HEAD_DIM
K_OFF = Q_COLS
V_OFF = K_OFF + KV_COLS
U_OFF = V_OFF + KV_COLS
VS_OFF = U_OFF + SG_WIDTH
GA_OFF = VS_OFF + SG_WIDTH
GS_OFF = GA_OFF + D_MODEL
N_IN = GS_OFF + D_MODEL

LANES = 128
SUBLANES = 8
VMEM_LIMIT_BYTES = 56 * 1024 * 1024

IN_TM = 1024
IN_TN = 1024
IN_SUB = 256
ATT_TQ = 1024
ATT_SUB = 256
SGU_TR = 512
MIX_TM = 256
HALF = 1024
XA_TM = 512
FFN_TM = 512
FFN_TF = 512
MEM_TM_MAX = 1024
MEM_TN = 512
HALO = SUBLANES


def _params(*sem):
    return pltpu.CompilerParams(dimension_semantics=sem, vmem_limit_bytes=VMEM_LIMIT_BYTES)


def _rms(xf, g):
    ms = jnp.mean(xf * xf, axis=-1, keepdims=True)
    return xf * lax.rsqrt(ms + EPS) * g


def _gelu_tanh(x):
    inner = x * (GELU_C + (GELU_C * 0.044715) * (x * x))
    return (0.5 * x) * (1.0 + jnp.tanh(inner))


def _split_rows(tm, first_rows):
    assert first_rows % tm == 0
    nf = first_rows // tm
    first = pl.BlockSpec((tm, D_MODEL), lambda i, *_: (jnp.minimum(i, nf - 1), 0))
    second = pl.BlockSpec((tm, D_MODEL), lambda i, *_: (jnp.maximum(i - nf, 0), 0))
    return first, second


def _resident(shape, index_map):
    return pl.BlockSpec(shape, index_map, pipeline_mode=pl.Buffered(1))


def _in_proj_kernel(x_ref, g_ref, w_ref, cos_ref, sina_ref, sinb_ref, qg_ref, kg_ref, *refs):
    o_ref, h_ref = refs[-2:]
    j = pl.program_id(1)
    tm = x_ref.shape[0]

    @pl.when(j == 0)
    def _():
        for r in range(0, tm, IN_SUB):
            h_ref[r:r + IN_SUB, :] = _rms(x_ref[r:r + IN_SUB, :], g_ref[...]).astype(BF16)

    def norm_rope(y, rows, n_heads, gain_ref):
        for hh in range(n_heads):
            cs = slice(hh * HEAD_DIM, (hh + 1) * HEAD_DIM)
            yn = _rms(y[:, cs], gain_ref[...])
            out = (yn * cos_ref[rows, :]
                   + pltpu.roll(yn, HEAD_DIM - HEAD_DIM // 4, 1) * sina_ref[rows, :]
                   + pltpu.roll(yn, HEAD_DIM // 4, 1) * sinb_ref[rows, :])
            o_ref[rows, cs] = out.astype(BF16)

    def run(epilogue):
        for r in range(0, tm, IN_SUB):
            rows = slice(r, r + IN_SUB)
            y = jnp.dot(h_ref[rows, :], w_ref[...], preferred_element_type=F32)
            epilogue(y, rows)

    def store(fn):
        def ep(y, rows):
            o_ref[rows, :] = fn(y).astype(BF16)
        return ep

    @pl.when(j < K_OFF // IN_TN)
    def _():
        run(lambda y, rows: norm_rope(y, rows, IN_TN // HEAD_DIM, qg_ref))

    @pl.when(j == K_OFF // IN_TN)
    def _():
        def ep(y, rows):
            norm_rope(y, rows, N_KV_HEADS, kg_ref)
            o_ref[rows, KV_COLS:] = y[:, KV_COLS:].astype(BF16)
        run(ep)

    @pl.when((j >= U_OFF // IN_TN) & (j < GA_OFF // IN_TN))
    def _():
        run(store(_gelu_tanh))

    @pl.when(j >= GA_OFF // IN_TN)
    def _():
        run(store(jax.nn.sigmoid))


def _in_proj(x, total_rows, row_off, y_partial, g, w_in, layer, cos, sina, sinb, qg, kg, seq):
    tile_off = row_off // IN_TM
    tiles_per_seq = seq // IN_TM
    rope_spec = pl.BlockSpec((IN_TM, HEAD_DIM), lambda i, j: (i % tiles_per_seq, 0))
    vec = lambda n: pl.BlockSpec((1, n), lambda i, j: (0, 0))
    in_specs = [
        pl.BlockSpec((IN_TM, D_MODEL), lambda i, j: (i, 0)),
        vec(D_MODEL),
        pl.BlockSpec((None, D_MODEL, IN_TN), lambda i, j: (layer, 0, j)),
        rope_spec, rope_spec, rope_spec,
        vec(HEAD_DIM), vec(HEAD_DIM),
    ]
    args = [x, g, w_in, cos, sina, sinb, qg, kg]
    aliases = {}
    if y_partial is not None:
        aliases = {len(args): 0}
        in_specs.append(pl.BlockSpec(memory_space=pl.ANY))
        args.append(y_partial)
    return pl.pallas_call(
        _in_proj_kernel,
        grid=(x.shape[0] // IN_TM, N_IN // IN_TN),
        in_specs=in_specs,
        out_specs=pl.BlockSpec((IN_TM, IN_TN), lambda i, j: (tile_off + i, j)),
        out_shape=jax.ShapeDtypeStruct((total_rows, N_IN), BF16),
        scratch_shapes=[pltpu.VMEM((IN_TM, D_MODEL), BF16)],
        input_output_aliases=aliases,
        compiler_params=_params("parallel", "arbitrary"),
        name="in_proj",
    )(*args)


def _rms_matmul_kernel(x_ref, g_ref, w_ref, o_ref, h_ref):
    @pl.when(pl.program_id(1) == 0)
    def _():
        h_ref[...] = _rms(x_ref[...], g_ref[...]).astype(BF16)

    o_ref[...] = jnp.dot(h_ref[...], w_ref[...], preferred_element_type=F32).astype(BF16)


def _rms_matmul(x, g, w, layer, tm, tn):
    t = x.shape[0]
    n = w.shape[-1]
    return pl.pallas_call(
        _rms_matmul_kernel,
        grid=(t // tm, n // tn),
        in_specs=[
            pl.BlockSpec((tm, D_MODEL), lambda i, j: (i, 0)),
            pl.BlockSpec((1, D_MODEL), lambda i, j: (0, 0)),
            pl.BlockSpec((None, D_MODEL, tn), lambda i, j: (layer, 0, j)),
        ],
        out_specs=pl.BlockSpec((tm, tn), lambda i, j: (i, j)),
        out_shape=jax.ShapeDtypeStruct((t, n), BF16),
        scratch_shapes=[pltpu.VMEM((tm, D_MODEL), BF16)],
        compiler_params=_params("parallel", "arbitrary"),
        name="mem_kv_proj",
    )(x, g, w)


def _attention_kernel(q_ref, k_ref, v_ref, o_ref, kt_ref):
    @pl.when(pl.program_id(2) == 0)
    def _():
        kt_ref[...] = k_ref[...].T

    v = v_ref[...]
    blocks = [(g, r) for g in range(GROUP) for r in range(0, ATT_TQ, ATT_SUB)]

    def scores(g, r):
        q = q_ref[r:r + ATT_SUB, g * HEAD_DIM:(g + 1) * HEAD_DIM]
        return jnp.dot(q, kt_ref[...], preferred_element_type=F32)

    s = scores(*blocks[0])
    for n, (g, r) in enumerate(blocks):
        s_next = scores(*blocks[n + 1]) if n + 1 < len(blocks) else None
        p = jnp.exp2(s - jnp.max(s, axis=-1, keepdims=True))
        l = jnp.sum(p, axis=-1, keepdims=True)
        o = jnp.dot(p.astype(BF16), v, preferred_element_type=F32)
        o_ref[r:r + ATT_SUB, g * HEAD_DIM:(g + 1) * HEAD_DIM] = (o / l).astype(BF16)
        s = s_next


def _attention(y, seq):
    t = y.shape[0]
    n_seq = t // seq
    q_tiles = seq // ATT_TQ
    gw = GROUP * HEAD_DIM
    return pl.pallas_call(
        _attention_kernel,
        grid=(n_seq, N_KV_HEADS, q_tiles),
        in_specs=[
            pl.BlockSpec((ATT_TQ, gw), lambda b, h, qi: (b * q_tiles + qi, h)),
            pl.BlockSpec((seq, HEAD_DIM), lambda b, h, qi: (b, K_OFF // HEAD_DIM + h)),
            pl.BlockSpec((seq, HEAD_DIM), lambda b, h, qi: (b, V_OFF // HEAD_DIM + h)),
        ],
        out_specs=pl.BlockSpec((ATT_TQ, gw), lambda b, h, qi: (b * q_tiles + qi, h)),
        out_shape=jax.ShapeDtypeStruct((t, Q_COLS), BF16),
        scratch_shapes=[pltpu.VMEM((HEAD_DIM, seq), BF16)],
        compiler_params=_params("parallel", "parallel", "arbitrary"),
        name="attention",
    )(y, y, y)


def _sgu_kernel(u0_ref, u1_ref, v0_ref, v1_ref, lg_ref, lb_ref, ws_ref, bs_ref, o_ref):
    groups_per_half = HALF // SG_GROUP_W
    for c in range(SGU_TR // SG_CHUNK):
        rows = slice(c * SG_CHUNK, (c + 1) * SG_CHUNK)
        v = jnp.concatenate([v0_ref[rows, :], v1_ref[rows, :]], axis=-1).astype(F32)
        xc = v - jnp.mean(v, axis=-1, keepdims=True)
        var = jnp.mean(xc * xc, axis=-1, keepdims=True)
        vn = (xc * lax.rsqrt(var + EPS) * lg_ref[...] + lb_ref[...]).astype(BF16)
        for g in range(SG_GROUPS):
            cs = slice(g * SG_GROUP_W, (g + 1) * SG_GROUP_W)
            mixed = jnp.dot(ws_ref[g], vn[:, cs], preferred_element_type=F32) + bs_ref[:, g:g + 1]
            u_ref = u0_ref if g < groups_per_half else u1_ref
            gl = g % groups_per_half
            u = u_ref[rows, gl * SG_GROUP_W:(gl + 1) * SG_GROUP_W].astype(F32)
            o_ref[rows, cs] = (u * mixed).astype(BF16)


def _sgu(y, lg, lb, ws, bs_t):
    t = y.shape[0]
    half = lambda off: pl.BlockSpec((SGU_TR, HALF), lambda i: (i, off // HALF))
    vec = pl.BlockSpec((1, SG_WIDTH), lambda i: (0, 0))
    return pl.pallas_call(
        _sgu_kernel,
        grid=(t // SGU_TR,),
        in_specs=[
            half(U_OFF), half(U_OFF + HALF), half(VS_OFF), half(VS_OFF + HALF),
            vec, vec,
            pl.BlockSpec((SG_GROUPS, SG_CHUNK, SG_CHUNK), lambda i: (0, 0, 0)),
            pl.BlockSpec((SG_CHUNK, SG_GROUPS), lambda i: (0, 0)),
        ],
        out_specs=pl.BlockSpec((SGU_TR, SG_WIDTH), lambda i: (i, 0)),
        out_shape=jax.ShapeDtypeStruct((t, SG_WIDTH), BF16),
        compiler_params=_params("parallel"),
        name="spatial_gating",
    )(y, y, y, y, lg, lb, ws, bs_t)


def _mix_kernel(a_ref, s_ref, ga0_ref, ga1_ref, gs0_ref, gs1_ref, xa_ref, xb_ref,
                wa_ref, ws_ref, wo_ref, o_ref, mix_ref, *, first_tiles):
    a_in = a_ref[...]
    s_in = s_ref[...]
    for n, (ga_ref, gs_ref) in enumerate(((ga0_ref, gs0_ref), (ga1_ref, gs1_ref))):
        cs = slice(n * HALF, (n + 1) * HALF)
        br_a = jnp.dot(a_in, wa_ref[:, cs], preferred_element_type=F32)
        br_s = jnp.dot(s_in, ws_ref[:, cs], preferred_element_type=F32)
        mix = ga_ref[...].astype(F32) * br_a + gs_ref[...].astype(F32) * br_s
        mix_ref[:, cs] = mix.astype(BF16)
    for n in range(D_MODEL // HALF):
        cs = slice(n * HALF, (n + 1) * HALF)
        x = jnp.where(pl.program_id(0) < first_tiles, xa_ref[:, cs], xb_ref[:, cs])
        o_ref[:, cs] = x + jnp.dot(mix_ref[...], wo_ref[:, cs], preferred_element_type=F32)


def _mix(att, sg, y, xa, xb, first_rows, wa, ws, wo, layer):
    t = att.shape[0]
    xa_spec, xb_spec = _split_rows(MIX_TM, first_rows)
    row = lambda w: pl.BlockSpec((MIX_TM, w), lambda i: (i, 0))
    half = lambda off: pl.BlockSpec((MIX_TM, HALF), lambda i: (i, off // HALF))
    weight = _resident((None, D_MODEL, D_MODEL), lambda i: (layer, 0, 0))
    return pl.pallas_call(
        functools.partial(_mix_kernel, first_tiles=first_rows // MIX_TM),
        grid=(t // MIX_TM,),
        in_specs=[
            row(Q_COLS), row(SG_WIDTH),
            half(GA_OFF), half(GA_OFF + HALF), half(GS_OFF), half(GS_OFF + HALF),
            xa_spec, xb_spec,
            weight, weight, weight,
        ],
        out_specs=row(D_MODEL),
        out_shape=jax.ShapeDtypeStruct((t, D_MODEL), F32),
        scratch_shapes=[pltpu.VMEM((MIX_TM, D_MODEL), BF16)],
        compiler_params=_params("parallel"),
        name="branch_mix",
    )(att, sg, y, y, y, y, xa, xb, wa, ws, wo)


def _xattn_kernel(x_ref, g_ref, wq_ref, k_ref, v_ref, wo_ref, o_ref, h_ref, att_ref):
    h_ref[...] = _rms(x_ref[...], g_ref[...]).astype(BF16)
    scale = X_HEAD_DIM ** -0.5
    heads = [slice(hd * X_HEAD_DIM, (hd + 1) * X_HEAD_DIM) for hd in range(X_HEADS)]

    def q_proj(cs):
        return jnp.dot(h_ref[...], wq_ref[:, cs], preferred_element_type=F32).astype(BF16)

    q = q_proj(heads[0])
    for n, cs in enumerate(heads):
        s = lax.dot_general(q, k_ref[:, cs], (((1,), (1,)), ((), ())),
                            preferred_element_type=F32) * scale
        q = q_proj(heads[n + 1]) if n + 1 < len(heads) else None
        p = jnp.exp(s - jnp.max(s, axis=-1, keepdims=True))
        l = jnp.sum(p, axis=-1, keepdims=True)
        o = jnp.dot(p.astype(BF16), v_ref[:, cs], preferred_element_type=F32)
        att_ref[:, cs] = (o / l).astype(BF16)
    for n in range(D_MODEL // HALF):
        cs = slice(n * HALF, (n + 1) * HALF)
        o_ref[:, cs] = x_ref[:, cs] + jnp.dot(att_ref[...], wo_ref[:, cs],
                                              preferred_element_type=F32)


def _xattn(x, g, wq, kv, wo, layer, seq):
    t = x.shape[0]
    n_mem = kv.shape[0] // (t // seq)
    tiles_per_seq = seq // XA_TM
    row = pl.BlockSpec((XA_TM, D_MODEL), lambda i: (i, 0))
    weight = _resident((None, D_MODEL, D_MODEL), lambda i: (layer, 0, 0))
    return pl.pallas_call(
        _xattn_kernel,
        grid=(t // XA_TM,),
        in_specs=[
            row,
            pl.BlockSpec((1, D_MODEL), lambda i: (0, 0)),
            weight,
            pl.BlockSpec((n_mem, D_MODEL), lambda i: (i // tiles_per_seq, 0)),
            pl.BlockSpec((n_mem, D_MODEL), lambda i: (i // tiles_per_seq, 1)),
            weight,
        ],
        out_specs=row,
        out_shape=jax.ShapeDtypeStruct((t, D_MODEL), F32),
        scratch_shapes=[pltpu.VMEM((XA_TM, D_MODEL), BF16), pltpu.VMEM((XA_TM, D_MODEL), BF16)],
        compiler_params=_params("parallel"),
        name="memory_attention",
    )(x, g, wq, kv, kv, wo)


def _ffn_kernel(x_ref, xp_ref, xn_ref, g_ref, wa_ref, wb_ref, cw_ref, cb_ref, wd_ref, fg_ref,
                *refs, tiles_per_seq, split_tiles):
    if split_tiles is None:
        o_ref, h_ref = refs
        acc_ref = o_ref
    else:
        oa_ref, ob_ref, h_ref, acc_ref = refs
    i = pl.program_id(0)
    f = pl.program_id(1)
    tm = x_ref.shape[0]
    ext = tm + 2 * HALO

    @pl.when(f == 0)
    def _():
        pos = i % tiles_per_seq
        hp = _rms(xp_ref[...], g_ref[...])
        hn = _rms(xn_ref[...], g_ref[...])
        h_ref[:tm, :] = _rms(x_ref[...], g_ref[...]).astype(BF16)
        h_ref[tm:tm + HALO, :] = jnp.where(pos != tiles_per_seq - 1, hn, 0.0).astype(BF16)
        h_ref[tm + HALO:, :] = jnp.where(pos != 0, hp, 0.0).astype(BF16)
        acc_ref[...] = x_ref[...]

    a_ext = jnp.dot(h_ref[...], wa_ref[...], preferred_element_type=F32)
    b = jnp.dot(h_ref[:tm, :], wb_ref[...], preferred_element_type=F32)
    a_prev = pltpu.roll(a_ext, 1, 0)[:tm]
    a_next = pltpu.roll(a_ext, ext - 1, 0)[:tm]
    conv = (cb_ref[...] + a_prev * cw_ref[0:1, :] + a_ext[:tm] * cw_ref[1:2, :]
            + a_next * cw_ref[2:3, :])
    y = (_gelu_tanh(conv) * b).astype(BF16)
    acc_ref[...] += jnp.dot(y, wd_ref[...], preferred_element_type=F32)

    if split_tiles is not None:
        last = f == pl.num_programs(1) - 1

        @pl.when(last & (i < split_tiles))
        def _():
            oa_ref[...] = _rms(acc_ref[...], fg_ref[...])

        @pl.when(last & (i >= split_tiles))
        def _():
            ob_ref[...] = _rms(acc_ref[...], fg_ref[...])


def _ffn(x, g, w_up, conv_w, conv_b, w_down, fg, layer, seq, split_rows):
    t = x.shape[0]
    nf = D_FF // FFN_TF
    tiles_per_seq = seq // FFN_TM
    halo_blocks = FFN_TM // HALO
    last_halo = t // HALO - 1
    row = pl.BlockSpec((FFN_TM, D_MODEL), lambda i, f: (i, 0))
    vec = pl.BlockSpec((1, D_MODEL), lambda i, f: (0, 0))
    scratch = [pltpu.VMEM((FFN_TM + 2 * HALO, D_MODEL), BF16)]
    if split_rows is None:
        split_tiles = None
        out_specs = row
        out_shape = jax.ShapeDtypeStruct((t, D_MODEL), F32)
    else:
        split_tiles = split_rows // FFN_TM
        out_specs = list(_split_rows(FFN_TM, split_rows))
        out_shape = [jax.ShapeDtypeStruct((split_rows, D_MODEL), F32),
                     jax.ShapeDtypeStruct((t - split_rows, D_MODEL), F32)]
        scratch.append(pltpu.VMEM((FFN_TM, D_MODEL), F32))
    kern = functools.partial(_ffn_kernel, tiles_per_seq=tiles_per_seq, split_tiles=split_tiles)
    return pl.pallas_call(
        kern,
        grid=(t // FFN_TM, nf),
        in_specs=[
            row,
            pl.BlockSpec((HALO, D_MODEL), lambda i, f: (jnp.maximum(i * halo_blocks - 1, 0), 0)),
            pl.BlockSpec((HALO, D_MODEL),
                         lambda i, f: (jnp.minimum((i + 1) * halo_blocks, last_halo), 0)),
            vec,
            pl.BlockSpec((None, D_MODEL, FFN_TF), lambda i, f: (layer, 0, f)),
            pl.BlockSpec((None, D_MODEL, FFN_TF), lambda i, f: (layer, 0, nf + f)),
            pl.BlockSpec((3, FFN_TF), lambda i, f: (0, f)),
            pl.BlockSpec((1, FFN_TF), lambda i, f: (0, f)),
            pl.BlockSpec((None, FFN_TF, D_MODEL), lambda i, f: (layer, f, 0)),
            vec,
        ],
        out_specs=out_specs,
        out_shape=out_shape,
        scratch_shapes=scratch,
        compiler_params=_params("parallel" if split_rows is None else "arbitrary", "arbitrary"),
        name="conv_ffn",
    )(x, x, x, g, w_up, w_up, conv_w, conv_b, w_down, fg)


def _rope_tables(seq):
    t = jnp.arange(seq)
    row = (t // GRID_W).astype(F32)
    col = (t % GRID_W).astype(F32)
    quarter = HEAD_DIM // 4
    inv = ROPE_BASE ** (-jnp.arange(quarter, dtype=F32) / quarter)
    ang_r = row[:, None] * inv[None, :]
    ang_c = col[:, None] * inv[None, :]
    ang = jnp.concatenate([ang_r, ang_r, ang_c, ang_c], axis=-1)
    cos, sin = jnp.cos(ang), jnp.sin(ang)
    first = (jnp.arange(HEAD_DIM) % (2 * quarter)) < quarter
    return cos, jnp.where(first, -sin, 0.0), jnp.where(first, 0.0, sin)


def kernel(x_prompt, x_sample, mem_prompt, mem_sample, norm_mix_g, w_in, q_norm_g, k_norm_g, w_attn_o, sg_norm_g, sg_norm_b, w_spatial, b_spatial, w_sg_o, w_out, norm_x_g, norm_mem_g, w_xq, w_xkv, w_xo, norm_ffn_g, w_ffn_up, conv_w, conv_b, w_ffn_down, final_norm_g):
    depth = w_in.shape[0]
    seq = x_prompt.shape[1]
    assert x_sample.shape[1] == seq and seq % IN_TM == 0 and seq % GRID_W == 0
    assert w_in.shape[1:] == (D_MODEL, N_IN) and w_ffn_up.shape[2] == 2 * D_FF
    n_prompt = x_prompt.shape[0] * seq

    xa, xb, first_rows = x_prompt.reshape(-1, D_MODEL), x_sample.reshape(-1, D_MODEL), n_prompt
    total_rows = n_prompt + x_sample.shape[0] * seq
    mem = jnp.concatenate([mem_prompt.reshape(-1, D_MODEL), mem_sample.reshape(-1, D_MODEL)], axis=0)

    cos, sina, sinb = _rope_tables(seq)
    row = lambda v: v.reshape(1, -1).astype(F32)
    bf = lambda w: w.astype(BF16)
    w_in_b, w_attn_o_b, w_sg_o_b, w_out_b = bf(w_in), bf(w_attn_o), bf(w_sg_o), bf(w_out)
    w_xq_b, w_xkv_b, w_xo_b = bf(w_xq), bf(w_xkv), bf(w_xo)
    w_up_b, w_down_b, w_sp_b = bf(w_ffn_up), bf(w_ffn_down), bf(w_spatial)
    q_scale = HEAD_DIM ** -0.5 * LOG2_E
    fg = row(final_norm_g)
    mem_tm = MEM_TM_MAX
    while mem.shape[0] % mem_tm:
        mem_tm //= 2
    assert mem_tm % SUBLANES == 0

    for l in range(depth):
        y = None
        for part, off in ((xa, 0), (xb, first_rows)) if xb is not xa else ((xa, 0),):
            y = _in_proj(part, total_rows, off, y, row(norm_mix_g[l]), w_in_b, l, cos, sina, sinb,
                         row(q_norm_g[l]) * q_scale, row(k_norm_g[l]), seq)
        att = _attention(y, seq)
        sg = _sgu(y, row(sg_norm_g[l]), row(sg_norm_b[l]), w_sp_b[l], b_spatial[l].T)
        x = _mix(att, sg, y, xa, xb, first_rows, w_attn_o_b, w_sg_o_b, w_out_b, l)
        kv = _rms_matmul(mem, row(norm_mem_g[l]), w_xkv_b, l, mem_tm, MEM_TN)
        x = _xattn(x, row(norm_x_g[l]), w_xq_b, kv, w_xo_b, l, seq)
        x = _ffn(x, row(norm_ffn_g[l]), w_up_b, conv_w[l], row(conv_b[l]), w_down_b, fg,
                 l, seq, n_prompt if l == depth - 1 else None)
        xa, xb, first_rows = x, x, total_rows

    y_prompt, y_sample = x
    return (y_prompt.reshape(x_prompt.shape), y_sample.reshape(x_sample.shape))
```

```python
import functools

import jax
import jax.numpy as jnp
from jax import lax
from jax.experimental import pallas as pl
from jax.experimental.pallas import tpu as pltpu

F32 = jnp.float32
BF16 = jnp.bfloat16

D_MODEL = 2048
GRID_W = 64
HEAD_DIM = 128
N_HEADS = D_MODEL // HEAD_DIM
N_KV_HEADS = N_HEADS // 4
GROUP = N_HEADS // N_KV_HEADS
ROPE_BASE = 10000.0
SG_WIDTH = D_MODEL
SG_GROUPS = 8
SG_CHUNK = 128
SG_GROUP_W = SG_WIDTH // SG_GROUPS
X_HEADS = 4
X_HEAD_DIM = D_MODEL // X_HEADS
D_FF = 5632
EPS = 1e-6
LOG2_E = 1.4426950408889634
GELU_C = 0.7978845608028654

Q_COLS = N_HEADS * HEAD_DIM
KV_COLS = N_KV_HEADS * HEAD_DIM
K_OFF = Q_COLS
V_OFF = K_OFF + KV_COLS
U_OFF = V_OFF + KV_COLS
VS_OFF = U_OFF + SG_WIDTH
GA_OFF = VS_OFF + SG_WIDTH
GS_OFF = GA_OFF + D_MODEL
N_IN = GS_OFF + D_MODEL

LANES = 128
SUBLANES = 8
VMEM_LIMIT_BYTES = 56 * 1024 * 1024

IN_TM = 1024
IN_TN = 1024
IN_SUB = 256
ATT_TQ = 1024
ATT_SUB = 256
SGU_TR = 512
MIX_TM = 256
HALF = 1024
XA_TM = 512
FFN_TM = 512
FFN_TF = 512
MEM_TM_MAX = 1024
MEM_TN = 512
HALO = SUBLANES


def _params(*sem):
    return pltpu.CompilerParams(dimension_semantics=sem, vmem_limit_bytes=VMEM_LIMIT_BYTES)


def _rms(xf, g):
    ms = jnp.mean(xf * xf, axis=-1, keepdims=True)
    return xf * lax.rsqrt(ms + EPS) * g


def _gelu_tanh(x):
    inner = x * (GELU_C + (GELU_C * 0.044715) * (x * x))
    return (0.5 * x) * (1.0 + jnp.tanh(inner))


def _split_rows(tm, first_rows):
    assert first_rows % tm == 0
    nf = first_rows // tm
    first = pl.BlockSpec((tm, D_MODEL), lambda i, *_: (jnp.minimum(i, nf - 1), 0))
    second = pl.BlockSpec((tm, D_MODEL), lambda i, *_: (jnp.maximum(i - nf, 0), 0))
    return first, second


def _resident(shape, index_map):
    return pl.BlockSpec(shape, index_map, pipeline_mode=pl.Buffered(1))


def _in_proj_kernel(x_ref, g_ref, w_ref, cos_ref, sina_ref, sinb_ref, qg_ref, kg_ref, *refs):
    o_ref, h_ref = refs[-2:]
    j = pl.program_id(1)
    tm = x_ref.shape[0]

    @pl.when(j == 0)
    def _():
        for r in range(0, tm, IN_SUB):
            h_ref[r:r + IN_SUB, :] = _rms(x_ref[r:r + IN_SUB, :], g_ref[...]).astype(BF16)

    def norm_rope(y, rows, n_heads, gain_ref):
        for hh in range(n_heads):
            cs = slice(hh * HEAD_DIM, (hh + 1) * HEAD_DIM)
            yn = _rms(y[:, cs], gain_ref[...])
            out = (yn * cos_ref[rows, :]
                   + pltpu.roll(yn, HEAD_DIM - HEAD_DIM // 4, 1) * sina_ref[rows, :]
                   + pltpu.roll(yn, HEAD_DIM // 4, 1) * sinb_ref[rows, :])
            o_ref[rows, cs] = out.astype(BF16)

    def run(epilogue):
        for r in range(0, tm, IN_SUB):
            rows = slice(r, r + IN_SUB)
            y = jnp.dot(h_ref[rows, :], w_ref[...], preferred_element_type=F32)
            epilogue(y, rows)

    def store(fn):
        def ep(y, rows):
            o_ref[rows, :] = fn(y).astype(BF16)
        return ep

    @pl.when(j < K_OFF // IN_TN)
    def _():
        run(lambda y, rows: norm_rope(y, rows, IN_TN // HEAD_DIM, qg_ref))

    @pl.when(j == K_OFF // IN_TN)
    def _():
        def ep(y, rows):
            norm_rope(y, rows, N_KV_HEADS, kg_ref)
            o_ref[rows, KV_COLS:] = y[:, KV_COLS:].astype(BF16)
        run(ep)

    @pl.when((j >= U_OFF // IN_TN) & (j < GA_OFF // IN_TN))
    def _():
        run(store(_gelu_tanh))

    @pl.when(j >= GA_OFF // IN_TN)
    def _():
        run(store(jax.nn.sigmoid))


def _in_proj(x, total_rows, row_off, y_partial, g, w_in, layer, cos, sina, sinb, qg, kg, seq):
    tile_off = row_off // IN_TM
    tiles_per_seq = seq // IN_TM
    rope_spec = pl.BlockSpec((IN_TM, HEAD_DIM), lambda i, j: (i % tiles_per_seq, 0))
    vec = lambda n: pl.BlockSpec((1, n), lambda i, j: (0, 0))
    in_specs = [
        pl.BlockSpec((IN_TM, D_MODEL), lambda i, j: (i, 0)),
        vec(D_MODEL),
        pl.BlockSpec((None, D_MODEL, IN_TN), lambda i, j: (layer, 0, j)),
        rope_spec, rope_spec, rope_spec,
        vec(HEAD_DIM), vec(HEAD_DIM),
    ]
    args = [x, g, w_in, cos, sina, sinb, qg, kg]
    aliases = {}
    if y_partial is not None:
        aliases = {len(args): 0}
        in_specs.append(pl.BlockSpec(memory_space=pl.ANY))
        args.append(y_partial)
    return pl.pallas_call(
        _in_proj_kernel,
        grid=(x.shape[0] // IN_TM, N_IN // IN_TN),
        in_specs=in_specs,
        out_specs=pl.BlockSpec((IN_TM, IN_TN), lambda i, j: (tile_off + i, j)),
        out_shape=jax.ShapeDtypeStruct((total_rows, N_IN), BF16),
        scratch_shapes=[pltpu.VMEM((IN_TM, D_MODEL), BF16)],
        input_output_aliases=aliases,
        compiler_params=_params("parallel", "arbitrary"),
        name="in_proj",
    )(*args)


def _rms_matmul_kernel(x_ref, g_ref, w_ref, o_ref, h_ref):
    @pl.when(pl.program_id(1) == 0)
    def _():
        h_ref[...] = _rms(x_ref[...], g_ref[...]).astype(BF16)

    o_ref[...] = jnp.dot(h_ref[...], w_ref[...], preferred_element_type=F32).astype(BF16)


def _rms_matmul(x, g, w, layer, tm, tn):
    t = x.shape[0]
    n = w.shape[-1]
    return pl.pallas_call(
        _rms_matmul_kernel,
        grid=(t // tm, n // tn),
        in_specs=[
            pl.BlockSpec((tm, D_MODEL), lambda i, j: (i, 0)),
            pl.BlockSpec((1, D_MODEL), lambda i, j: (0, 0)),
            pl.BlockSpec((None, D_MODEL, tn), lambda i, j: (layer, 0, j)),
        ],
        out_specs=pl.BlockSpec((tm, tn), lambda i, j: (i, j)),
        out_shape=jax.ShapeDtypeStruct((t, n), BF16),
        scratch_shapes=[pltpu.VMEM((tm, D_MODEL), BF16)],
        compiler_params=_params("parallel", "arbitrary"),
        name="mem_kv_proj",
    )(x, g, w)


def _attention_kernel(q_ref, k_ref, v_ref, o_ref, kt_ref):
    @pl.when(pl.program_id(2) == 0)
    def _():
        kt_ref[...] = k_ref[...].T

    v = v_ref[...]
    blocks = [(g, r) for g in range(GROUP) for r in range(0, ATT_TQ, ATT_SUB)]

    def scores(g, r):
        q = q_ref[r:r + ATT_SUB, g * HEAD_DIM:(g + 1) * HEAD_DIM]
        return jnp.dot(q, kt_ref[...], preferred_element_type=F32)

    s = scores(*blocks[0])
    for n, (g, r) in enumerate(blocks):
        s_next = scores(*blocks[n + 1]) if n + 1 < len(blocks) else None
        p = jnp.exp2(s - jnp.max(s, axis=-1, keepdims=True))
        l = jnp.sum(p, axis=-1, keepdims=True)
        o = jnp.dot(p.astype(BF16), v, preferred_element_type=F32)
        o_ref[r:r + ATT_SUB, g * HEAD_DIM:(g + 1) * HEAD_DIM] = (o / l).astype(BF16)
        s = s_next


def _attention(y, seq):
    t = y.shape[0]
    n_seq = t // seq
    q_tiles = seq // ATT_TQ
    gw = GROUP * HEAD_DIM
    return pl.pallas_call(
        _attention_kernel,
        grid=(n_seq, N_KV_HEADS, q_tiles),
        in_specs=[
            pl.BlockSpec((ATT_TQ, gw), lambda b, h, qi: (b * q_tiles + qi, h)),
            pl.BlockSpec((seq, HEAD_DIM), lambda b, h, qi: (b, K_OFF // HEAD_DIM + h)),
            pl.BlockSpec((seq, HEAD_DIM), lambda b, h, qi: (b, V_OFF // HEAD_DIM + h)),
        ],
        out_specs=pl.BlockSpec((ATT_TQ, gw), lambda b, h, qi: (b * q_tiles + qi, h)),
        out_shape=jax.ShapeDtypeStruct((t, Q_COLS), BF16),
        scratch_shapes=[pltpu.VMEM((HEAD_DIM, seq), BF16)],
        compiler_params=_params("parallel", "parallel", "arbitrary"),
        name="attention",
    )(y, y, y)


def _sgu_kernel(u0_ref, u1_ref, v0_ref, v1_ref, lg_ref, lb_ref, ws_ref, bs_ref, o_ref):
    groups_per_half = HALF // SG_GROUP_W
    for c in range(SGU_TR // SG_CHUNK):
        rows = slice(c * SG_CHUNK, (c + 1) * SG_CHUNK)
        v = jnp.concatenate([v0_ref[rows, :], v1_ref[rows, :]], axis=-1).astype(F32)
        xc = v - jnp.mean(v, axis=-1, keepdims=True)
        var = jnp.mean(xc * xc, axis=-1, keepdims=True)
        vn = (xc * lax.rsqrt(var + EPS) * lg_ref[...] + lb_ref[...]).astype(BF16)
        for g in range(SG_GROUPS):
            cs = slice(g * SG_GROUP_W, (g + 1) * SG_GROUP_W)
            mixed = jnp.dot(ws_ref[g], vn[:, cs], preferred_element_type=F32) + bs_ref[:, g:g + 1]
            u_ref = u0_ref if g < groups_per_half else u1_ref
            gl = g % groups_per_half
            u = u_ref[rows, gl * SG_GROUP_W:(gl + 1) * SG_GROUP_W].astype(F32)
            o_ref[rows, cs] = (u * mixed).astype(BF16)


def _sgu(y, lg, lb, ws, bs_t):
    t = y.shape[0]
    half = lambda off: pl.BlockSpec((SGU_TR, HALF), lambda i: (i, off // HALF))
    vec = pl.BlockSpec((1, SG_WIDTH), lambda i: (0, 0))
    return pl.pallas_call(
        _sgu_kernel,
        grid=(t // SGU_TR,),
        in_specs=[
            half(U_OFF), half(U_OFF + HALF), half(VS_OFF), half(VS_OFF + HALF),
            vec, vec,
            pl.BlockSpec((SG_GROUPS, SG_CHUNK, SG_CHUNK), lambda i: (0, 0, 0)),
            pl.BlockSpec((SG_CHUNK, SG_GROUPS), lambda i: (0, 0)),
        ],
        out_specs=pl.BlockSpec((SGU_TR, SG_WIDTH), lambda i: (i, 0)),
        out_shape=jax.ShapeDtypeStruct((t, SG_WIDTH), BF16),
        compiler_params=_params("parallel"),
        name="spatial_gating",
    )(y, y, y, y, lg, lb, ws, bs_t)


def _mix_kernel(a_ref, s_ref, ga0_ref, ga1_ref, gs0_ref, gs1_ref, xa_ref, xb_ref,
                wa_ref, ws_ref, wo_ref, o_ref, mix_ref, *, first_tiles):
    a_in = a_ref[...]
    s_in = s_ref[...]
    for n, (ga_ref, gs_ref) in enumerate(((ga0_ref, gs0_ref), (ga1_ref, gs1_ref))):
        cs = slice(n * HALF, (n + 1) * HALF)
        br_a = jnp.dot(a_in, wa_ref[:, cs], preferred_element_type=F32)
        br_s = jnp.dot(s_in, ws_ref[:, cs], preferred_element_type=F32)
        mix = ga_ref[...].astype(F32) * br_a + gs_ref[...].astype(F32) * br_s
        mix_ref[:, cs] = mix.astype(BF16)
    for n in range(D_MODEL // HALF):
        cs = slice(n * HALF, (n + 1) * HALF)
        x = jnp.where(pl.program_id(0) < first_tiles, xa_ref[:, cs], xb_ref[:, cs])
        o_ref[:, cs] = x + jnp.dot(mix_ref[...], wo_ref[:, cs], preferred_element_type=F32)


def _mix(att, sg, y, xa, xb, first_rows, wa, ws, wo, layer):
    t = att.shape[0]
    xa_spec, xb_spec = _split_rows(MIX_TM, first_rows)
    row = lambda w: pl.BlockSpec((MIX_TM, w), lambda i: (i, 0))
    half = lambda off: pl.BlockSpec((MIX_TM, HALF), lambda i: (i, off // HALF))
    weight = _resident((None, D_MODEL, D_MODEL), lambda i: (layer, 0, 0))
    return pl.pallas_call(
        functools.partial(_mix_kernel, first_tiles=first_rows // MIX_TM),
        grid=(t // MIX_TM,),
        in_specs=[
            row(Q_COLS), row(SG_WIDTH),
            half(GA_OFF), half(GA_OFF + HALF), half(GS_OFF), half(GS_OFF + HALF),
            xa_spec, xb_spec,
            weight, weight, weight,
        ],
        out_specs=row(D_MODEL),
        out_shape=jax.ShapeDtypeStruct((t, D_MODEL), F32),
        scratch_shapes=[pltpu.VMEM((MIX_TM, D_MODEL), BF16)],
        compiler_params=_params("parallel"),
        name="branch_mix",
    )(att, sg, y, y, y, y, xa, xb, wa, ws, wo)


def _mem_fold_kernel(k_ref, v_ref, wq_ref, wo_ref, qk_ref, vo_ref):
    n_mem = k_ref.shape[0]
    scale = X_HEAD_DIM ** -0.5
    for hd in range(X_HEADS):
        cs = slice(hd * X_HEAD_DIM, (hd + 1) * X_HEAD_DIM)
        ms = slice(hd * n_mem, (hd + 1) * n_mem)
        qk = lax.dot_general(wq_ref[:, cs], k_ref[:, cs], (((1,), (1,)), ((), ())),
                             preferred_element_type=F32)
        qk_ref[:, ms] = (qk * scale).astype(BF16)
        vo_ref[ms, :] = jnp.dot(v_ref[:, cs], wo_ref[cs, :],
                                preferred_element_type=F32).astype(BF16)


def _mem_fold(kv, wq, wo, layer, n_mem):
    n_seq = kv.shape[0] // n_mem
    weight = _resident((None, D_MODEL, D_MODEL), lambda b: (layer, 0, 0))
    return pl.pallas_call(
        _mem_fold_kernel,
        grid=(n_seq,),
        in_specs=[
            pl.BlockSpec((n_mem, D_MODEL), lambda b: (b, 0)),
            pl.BlockSpec((n_mem, D_MODEL), lambda b: (b, 1)),
            weight, weight,
        ],
        out_specs=[
            pl.BlockSpec((None, D_MODEL, X_HEADS * n_mem), lambda b: (b, 0, 0)),
            pl.BlockSpec((None, X_HEADS * n_mem, D_MODEL), lambda b: (b, 0, 0)),
        ],
        out_shape=[
            jax.ShapeDtypeStruct((n_seq, D_MODEL, X_HEADS * n_mem), BF16),
            jax.ShapeDtypeStruct((n_seq, X_HEADS * n_mem, D_MODEL), BF16),
        ],
        compiler_params=_params("parallel"),
        name="memory_fold",
    )(kv, kv, wq, wo)


def _xattn_kernel(x_ref, g_ref, qk_ref, vo_ref, o_ref, p_ref):
    n_mem = qk_ref.shape[1] // X_HEADS
    h = _rms(x_ref[...], g_ref[...]).astype(BF16)
    s = jnp.dot(h, qk_ref[...], preferred_element_type=F32)
    for hd in range(X_HEADS):
        ms = slice(hd * n_mem, (hd + 1) * n_mem)
        e = jnp.exp(s[:, ms] - jnp.max(s[:, ms], axis=-1, keepdims=True))
        p_ref[:, ms] = (e / jnp.sum(e, axis=-1, keepdims=True)).astype(BF16)
    for n in range(D_MODEL // HALF):
        cs = slice(n * HALF, (n + 1) * HALF)
        o_ref[:, cs] = x_ref[:, cs] + jnp.dot(p_ref[...], vo_ref[:, cs],
                                              preferred_element_type=F32)


def _xattn(x, g, qk, vo, seq):
    t = x.shape[0]
    tiles_per_seq = seq // XA_TM
    row = pl.BlockSpec((XA_TM, D_MODEL), lambda i: (i, 0))
    return pl.pallas_call(
        _xattn_kernel,
        grid=(t // XA_TM,),
        in_specs=[
            row,
            pl.BlockSpec((1, D_MODEL), lambda i: (0, 0)),
            pl.BlockSpec((None,) + qk.shape[1:], lambda i: (i // tiles_per_seq, 0, 0)),
            pl.BlockSpec((None,) + vo.shape[1:], lambda i: (i // tiles_per_seq, 0, 0)),
        ],
        out_specs=row,
        out_shape=jax.ShapeDtypeStruct((t, D_MODEL), F32),
        scratch_shapes=[pltpu.VMEM((XA_TM, qk.shape[2]), BF16)],
        compiler_params=_params("parallel"),
        name="memory_attention",
    )(x, g, qk, vo)


def _ffn_kernel(x_ref, xp_ref, xn_ref, g_ref, wa_ref, wb_ref, cw_ref, cb_ref, wd_ref, fg_ref,
                *refs, tiles_per_seq, split_tiles):
    if split_tiles is None:
        o_ref, h_ref = refs
        acc_ref = o_ref
    else:
        oa_ref, ob_ref, h_ref, acc_ref = refs
    i = pl.program_id(0)
    f = pl.program_id(1)
    tm = x_ref.shape[0]
    ext = tm + 2 * HALO

    @pl.when(f == 0)
    def _():
        pos = i % tiles_per_seq
        hp = _rms(xp_ref[...], g_ref[...])
        hn = _rms(xn_ref[...], g_ref[...])
        h_ref[:tm, :] = _rms(x_ref[...], g_ref[...]).astype(BF16)
        h_ref[tm:tm + HALO, :] = jnp.where(pos != tiles_per_seq - 1, hn, 0.0).astype(BF16)
        h_ref[tm + HALO:, :] = jnp.where(pos != 0, hp, 0.0).astype(BF16)
        acc_ref[...] = x_ref[...]

    a_ext = jnp.dot(h_ref[...], wa_ref[...], preferred_element_type=F32)
    b = jnp.dot(h_ref[:tm, :], wb_ref[...], preferred_element_type=F32)
    a_prev = pltpu.roll(a_ext, 1, 0)[:tm]
    a_next = pltpu.roll(a_ext, ext - 1, 0)[:tm]
    conv = (cb_ref[...] + a_prev * cw_ref[0:1, :] + a_ext[:tm] * cw_ref[1:2, :]
            + a_next * cw_ref[2:3, :])
    y = (_gelu_tanh(conv) * b).astype(BF16)
    acc_ref[...] += jnp.dot(y, wd_ref[...], preferred_element_type=F32)

    if split_tiles is not None:
        last = f == pl.num_programs(1) - 1

        @pl.when(last & (i < split_tiles))
        def _():
            oa_ref[...] = _rms(acc_ref[...], fg_ref[...])

        @pl.when(last & (i >= split_tiles))
        def _():
            ob_ref[...] = _rms(acc_ref[...], fg_ref[...])


def _ffn(x, g, w_up, conv_w, conv_b, w_down, fg, layer, seq, split_rows):
    t = x.shape[0]
    nf = D_FF // FFN_TF
    tiles_per_seq = seq // FFN_TM
    halo_blocks = FFN_TM // HALO
    last_halo = t // HALO - 1
    row = pl.BlockSpec((FFN_TM, D_MODEL), lambda i, f: (i, 0))
    vec = pl.BlockSpec((1, D_MODEL), lambda i, f: (0, 0))
    scratch = [pltpu.VMEM((FFN_TM + 2 * HALO, D_MODEL), BF16)]
    if split_rows is None:
        split_tiles = None
        out_specs = row
        out_shape = jax.ShapeDtypeStruct((t, D_MODEL), F32)
    else:
        split_tiles = split_rows // FFN_TM
        out_specs = list(_split_rows(FFN_TM, split_rows))
        out_shape = [jax.ShapeDtypeStruct((split_rows, D_MODEL), F32),
                     jax.ShapeDtypeStruct((t - split_rows, D_MODEL), F32)]
        scratch.append(pltpu.VMEM((FFN_TM, D_MODEL), F32))
    kern = functools.partial(_ffn_kernel, tiles_per_seq=tiles_per_seq, split_tiles=split_tiles)
    return pl.pallas_call(
        kern,
        grid=(t // FFN_TM, nf),
        in_specs=[
            row,
            pl.BlockSpec((HALO, D_MODEL), lambda i, f: (jnp.maximum(i * halo_blocks - 1, 0), 0)),
            pl.BlockSpec((HALO, D_MODEL),
                         lambda i, f: (jnp.minimum((i + 1) * halo_blocks, last_halo), 0)),
            vec,
            pl.BlockSpec((None, D_MODEL, FFN_TF), lambda i, f: (layer, 0, f)),
            pl.BlockSpec((None, D_MODEL, FFN_TF), lambda i, f: (layer, 0, nf + f)),
            pl.BlockSpec((3, FFN_TF), lambda i, f: (0, f)),
            pl.BlockSpec((1, FFN_TF), lambda i, f: (0, f)),
            pl.BlockSpec((None, FFN_TF, D_MODEL), lambda i, f: (layer, f, 0)),
            vec,
        ],
        out_specs=out_specs,
        out_shape=out_shape,
        scratch_shapes=scratch,
        compiler_params=_params("parallel" if split_rows is None else "arbitrary", "arbitrary"),
        name="conv_ffn",
    )(x, x, x, g, w_up, w_up, conv_w, conv_b, w_down, fg)


def _rope_tables(seq):
    t = jnp.arange(seq)
    row = (t // GRID_W).astype(F32)
    col = (t % GRID_W).astype(F32)
    quarter = HEAD_DIM // 4
    inv = ROPE_BASE ** (-jnp.arange(quarter, dtype=F32) / quarter)
    ang_r = row[:, None] * inv[None, :]
    ang_c = col[:, None] * inv[None, :]
    ang = jnp.concatenate([ang_r, ang_r, ang_c, ang_c], axis=-1)
    cos, sin = jnp.cos(ang), jnp.sin(ang)
    first = (jnp.arange(HEAD_DIM) % (2 * quarter)) < quarter
    return cos, jnp.where(first, -sin, 0.0), jnp.where(first, 0.0, sin)


def kernel(x_prompt, x_sample, mem_prompt, mem_sample, norm_mix_g, w_in, q_norm_g, k_norm_g, w_attn_o, sg_norm_g, sg_norm_b, w_spatial, b_spatial, w_sg_o, w_out, norm_x_g, norm_mem_g, w_xq, w_xkv, w_xo, norm_ffn_g, w_ffn_up, conv_w, conv_b, w_ffn_down, final_norm_g):
    depth = w_in.shape[0]
    seq = x_prompt.shape[1]
    assert x_sample.shape[1] == seq and seq % IN_TM == 0 and seq % GRID_W == 0
    assert w_in.shape[1:] == (D_MODEL, N_IN) and w_ffn_up.shape[2] == 2 * D_FF
    n_prompt = x_prompt.shape[0] * seq

    xa, xb, first_rows = x_prompt.reshape(-1, D_MODEL), x_sample.reshape(-1, D_MODEL), n_prompt
    total_rows = n_prompt + x_sample.shape[0] * seq
    mem = jnp.concatenate([mem_prompt.reshape(-1, D_MODEL), mem_sample.reshape(-1, D_MODEL)], axis=0)
    n_mem = mem_prompt.shape[1]
    assert mem_sample.shape[1] == n_mem

    cos, sina, sinb = _rope_tables(seq)
    row = lambda v: v.reshape(1, -1).astype(F32)
    bf = lambda w: w.astype(BF16)
    w_in_b, w_attn_o_b, w_sg_o_b, w_out_b = bf(w_in), bf(w_attn_o), bf(w_sg_o), bf(w_out)
    w_xq_b, w_xkv_b, w_xo_b = bf(w_xq), bf(w_xkv), bf(w_xo)
    w_up_b, w_down_b, w_sp_b = bf(w_ffn_up), bf(w_ffn_down), bf(w_spatial)
    q_scale = HEAD_DIM ** -0.5 * LOG2_E
    fg = row(final_norm_g)
    mem_tm = MEM_TM_MAX
    while mem.shape[0] % mem_tm:
        mem_tm //= 2
    assert mem_tm % SUBLANES == 0

    for l in range(depth):
        y = None
        for part, off in ((xa, 0), (xb, first_rows)) if xb is not xa else ((xa, 0),):
            y = _in_proj(part, total_rows, off, y, row(norm_mix_g[l]), w_in_b, l, cos, sina, sinb,
                         row(q_norm_g[l]) * q_scale, row(k_norm_g[l]), seq)
        att = _attention(y, seq)
        sg = _sgu(y, row(sg_norm_g[l]), row(sg_norm_b[l]), w_sp_b[l], b_spatial[l].T)
        x = _mix(att, sg, y, xa, xb, first_rows, w_attn_o_b, w_sg_o_b, w_out_b, l)
        kv = _rms_matmul(mem, row(norm_mem_g[l]), w_xkv_b, l, mem_tm, MEM_TN)
        qk, vo = _mem_fold(kv, w_xq_b, w_xo_b, l, n_mem)
        x = _xattn(x, row(norm_x_g[l]), qk, vo, seq)
        x = _ffn(x, row(norm_ffn_g[l]), w_up_b, conv_w[l], row(conv_b[l]), w_down_b, fg,
                 l, seq, n_prompt if l == depth - 1 else None)
        xa, xb, first_rows = x, x, total_rows

    y_prompt, y_sample = x
    return (y_prompt.reshape(x_prompt.shape), y_sample.reshape(x_sample.shape))
```

```python
import functools

import jax
import jax.numpy as jnp
from jax import lax
from jax.experimental import pallas as pl
from jax.experimental.pallas import tpu as pltpu

F32 = jnp.float32
BF16 = jnp.bfloat16

D_MODEL = 2048
GRID_W = 64
HEAD_DIM = 128
N_HEADS = D_MODEL // HEAD_DIM
N_KV_HEADS = N_HEADS // 4
GROUP = N_HEADS // N_KV_HEADS
ROPE_BASE = 10000.0
SG_WIDTH = D_MODEL
SG_GROUPS = 8
SG_CHUNK = 128
SG_GROUP_W = SG_WIDTH // SG_GROUPS
X_HEADS = 4
X_HEAD_DIM = D_MODEL // X_HEADS
D_FF = 5632
EPS = 1e-6
LOG2_E = 1.4426950408889634
GELU_C = 0.7978845608028654

Q_COLS = N_HEADS * HEAD_DIM
KV_COLS = N_KV_HEADS * HEAD_DIM
K_OFF = Q_COLS
V_OFF = K_OFF + KV_COLS
U_OFF = V_OFF + KV_COLS
VS_OFF = U_OFF + SG_WIDTH
GA_OFF = VS_OFF + SG_WIDTH
GS_OFF = GA_OFF + D_MODEL
N_IN = GS_OFF + D_MODEL

LANES = 128
SUBLANES = 8
VMEM_LIMIT_BYTES = 56 * 1024 * 1024

IN_TM = 1024
IN_TN = 1024
IN_SUB = 256
ATT_TQ = 1024
ATT_SUB = 256
MIX_TM = 256
HALF = 1024
XA_TM = 512
FFN_TM = 512
FFN_TF = 512
MEM_TM_MAX = 1024
MEM_TN = 512
HALO = SUBLANES


def _params(*sem):
    return pltpu.CompilerParams(dimension_semantics=sem, vmem_limit_bytes=VMEM_LIMIT_BYTES)


def _rms(xf, g):
    ms = jnp.mean(xf * xf, axis=-1, keepdims=True)
    return xf * lax.rsqrt(ms + EPS) * g


def _gelu_tanh(x):
    inner = x * (GELU_C + (GELU_C * 0.044715) * (x * x))
    return (0.5 * x) * (1.0 + jnp.tanh(inner))


def _split_rows(tm, first_rows):
    assert first_rows % tm == 0
    nf = first_rows // tm
    first = pl.BlockSpec((tm, D_MODEL), lambda i, *_: (jnp.minimum(i, nf - 1), 0))
    second = pl.BlockSpec((tm, D_MODEL), lambda i, *_: (jnp.maximum(i - nf, 0), 0))
    return first, second


def _resident(shape, index_map):
    return pl.BlockSpec(shape, index_map, pipeline_mode=pl.Buffered(1))


def _in_proj_kernel(x_ref, g_ref, w_ref, cos_ref, sina_ref, sinb_ref, qg_ref, kg_ref, *refs):
    o_ref, h_ref = refs[-2:]
    j = pl.program_id(1)
    tm = x_ref.shape[0]

    @pl.when(j == 0)
    def _():
        for r in range(0, tm, IN_SUB):
            h_ref[r:r + IN_SUB, :] = _rms(x_ref[r:r + IN_SUB, :], g_ref[...]).astype(BF16)

    def norm_rope(y, rows, n_heads, gain_ref):
        for hh in range(n_heads):
            cs = slice(hh * HEAD_DIM, (hh + 1) * HEAD_DIM)
            yn = _rms(y[:, cs], gain_ref[...])
            out = (yn * cos_ref[rows, :]
                   + pltpu.roll(yn, HEAD_DIM - HEAD_DIM // 4, 1) * sina_ref[rows, :]
                   + pltpu.roll(yn, HEAD_DIM // 4, 1) * sinb_ref[rows, :])
            o_ref[rows, cs] = out.astype(BF16)

    def run(epilogue):
        for r in range(0, tm, IN_SUB):
            rows = slice(r, r + IN_SUB)
            y = jnp.dot(h_ref[rows, :], w_ref[...], preferred_element_type=F32)
            epilogue(y, rows)

    def store(fn):
        def ep(y, rows):
            o_ref[rows, :] = fn(y).astype(BF16)
        return ep

    @pl.when(j < K_OFF // IN_TN)
    def _():
        run(lambda y, rows: norm_rope(y, rows, IN_TN // HEAD_DIM, qg_ref))

    @pl.when(j == K_OFF // IN_TN)
    def _():
        def ep(y, rows):
            norm_rope(y, rows, N_KV_HEADS, kg_ref)
            o_ref[rows, KV_COLS:] = y[:, KV_COLS:].astype(BF16)
        run(ep)

    @pl.when((j >= U_OFF // IN_TN) & (j < GA_OFF // IN_TN))
    def _():
        run(store(_gelu_tanh))

    @pl.when(j >= GA_OFF // IN_TN)
    def _():
        run(store(jax.nn.sigmoid))


def _in_proj(x, total_rows, row_off, y_partial, g, w_in, layer, cos, sina, sinb, qg, kg, seq):
    tile_off = row_off // IN_TM
    tiles_per_seq = seq // IN_TM
    rope_spec = pl.BlockSpec((IN_TM, HEAD_DIM), lambda i, j: (i % tiles_per_seq, 0))
    vec = lambda n: pl.BlockSpec((1, n), lambda i, j: (0, 0))
    in_specs = [
        pl.BlockSpec((IN_TM, D_MODEL), lambda i, j: (i, 0)),
        vec(D_MODEL),
        pl.BlockSpec((None, D_MODEL, IN_TN), lambda i, j: (layer, 0, j)),
        rope_spec, rope_spec, rope_spec,
        vec(HEAD_DIM), vec(HEAD_DIM),
    ]
    args = [x, g, w_in, cos, sina, sinb, qg, kg]
    aliases = {}
    if y_partial is not None:
        aliases = {len(args): 0}
        in_specs.append(pl.BlockSpec(memory_space=pl.ANY))
        args.append(y_partial)
    return pl.pallas_call(
        _in_proj_kernel,
        grid=(x.shape[0] // IN_TM, N_IN // IN_TN),
        in_specs=in_specs,
        out_specs=pl.BlockSpec((IN_TM, IN_TN), lambda i, j: (tile_off + i, j)),
        out_shape=jax.ShapeDtypeStruct((total_rows, N_IN), BF16),
        scratch_shapes=[pltpu.VMEM((IN_TM, D_MODEL), BF16)],
        input_output_aliases=aliases,
        compiler_params=_params("parallel", "arbitrary"),
        name="in_proj",
    )(*args)


def _rms_matmul_kernel(x_ref, g_ref, w_ref, o_ref, h_ref):
    @pl.when(pl.program_id(1) == 0)
    def _():
        h_ref[...] = _rms(x_ref[...], g_ref[...]).astype(BF16)

    o_ref[...] = jnp.dot(h_ref[...], w_ref[...], preferred_element_type=F32).astype(BF16)


def _rms_matmul(x, g, w, layer, tm, tn):
    t = x.shape[0]
    n = w.shape[-1]
    return pl.pallas_call(
        _rms_matmul_kernel,
        grid=(t // tm, n // tn),
        in_specs=[
            pl.BlockSpec((tm, D_MODEL), lambda i, j: (i, 0)),
            pl.BlockSpec((1, D_MODEL), lambda i, j: (0, 0)),
            pl.BlockSpec((None, D_MODEL, tn), lambda i, j: (layer, 0, j)),
        ],
        out_specs=pl.BlockSpec((tm, tn), lambda i, j: (i, j)),
        out_shape=jax.ShapeDtypeStruct((t, n), BF16),
        scratch_shapes=[pltpu.VMEM((tm, D_MODEL), BF16)],
        compiler_params=_params("parallel", "arbitrary"),
        name="mem_kv_proj",
    )(x, g, w)


def _attention_kernel(q_ref, k_ref, v_ref, o_ref, kt_ref):
    @pl.when(pl.program_id(2) == 0)
    def _():
        kt_ref[...] = k_ref[...].T

    v = v_ref[...]
    blocks = [(g, r) for g in range(GROUP) for r in range(0, ATT_TQ, ATT_SUB)]

    def scores(g, r):
        q = q_ref[r:r + ATT_SUB, g * HEAD_DIM:(g + 1) * HEAD_DIM]
        return jnp.dot(q, kt_ref[...], preferred_element_type=F32)

    s = scores(*blocks[0])
    for n, (g, r) in enumerate(blocks):
        s_next = scores(*blocks[n + 1]) if n + 1 < len(blocks) else None
        p = jnp.exp2(s - jnp.max(s, axis=-1, keepdims=True))
        l = jnp.sum(p, axis=-1, keepdims=True)
        o = jnp.dot(p.astype(BF16), v, preferred_element_type=F32)
        o_ref[r:r + ATT_SUB, g * HEAD_DIM:(g + 1) * HEAD_DIM] = (o / l).astype(BF16)
        s = s_next


def _attention(y, seq):
    t = y.shape[0]
    n_seq = t // seq
    q_tiles = seq // ATT_TQ
    gw = GROUP * HEAD_DIM
    return pl.pallas_call(
        _attention_kernel,
        grid=(n_seq, N_KV_HEADS, q_tiles),
        in_specs=[
            pl.BlockSpec((ATT_TQ, gw), lambda b, h, qi: (b * q_tiles + qi, h)),
            pl.BlockSpec((seq, HEAD_DIM), lambda b, h, qi: (b, K_OFF // HEAD_DIM + h)),
            pl.BlockSpec((seq, HEAD_DIM), lambda b, h, qi: (b, V_OFF // HEAD_DIM + h)),
        ],
        out_specs=pl.BlockSpec((ATT_TQ, gw), lambda b, h, qi: (b * q_tiles + qi, h)),
        out_shape=jax.ShapeDtypeStruct((t, Q_COLS), BF16),
        scratch_shapes=[pltpu.VMEM((HEAD_DIM, seq), BF16)],
        compiler_params=_params("parallel", "parallel", "arbitrary"),
        name="attention",
    )(y, y, y)


def _mix_kernel(a_ref, u0_ref, u1_ref, v0_ref, v1_ref, lg_ref, lb_ref, wsp_ref, bs_ref,
                ga0_ref, ga1_ref, gs0_ref, gs1_ref, xa_ref, xb_ref,
                wa_ref, ws_ref, wo_ref, o_ref, sg_ref, mix_ref, *, first_tiles):
    halves = [slice(n * HALF, (n + 1) * HALF) for n in range(D_MODEL // HALF)]
    a_in = a_ref[...]
    br_a = [jnp.dot(a_in, wa_ref[:, cs], preferred_element_type=F32) for cs in halves]

    groups_per_half = HALF // SG_GROUP_W
    for c in range(MIX_TM // SG_CHUNK):
        rows = slice(c * SG_CHUNK, (c + 1) * SG_CHUNK)
        v = jnp.concatenate([v0_ref[rows, :], v1_ref[rows, :]], axis=-1).astype(F32)
        xc = v - jnp.mean(v, axis=-1, keepdims=True)
        var = jnp.mean(xc * xc, axis=-1, keepdims=True)
        vn = (xc * lax.rsqrt(var + EPS) * lg_ref[...] + lb_ref[...]).astype(BF16)
        for g in range(SG_GROUPS):
            cs = slice(g * SG_GROUP_W, (g + 1) * SG_GROUP_W)
            mixed = jnp.dot(wsp_ref[g], vn[:, cs], preferred_element_type=F32) + bs_ref[:, g:g + 1]
            u_ref = u0_ref if g < groups_per_half else u1_ref
            gl = g % groups_per_half
            u = u_ref[rows, gl * SG_GROUP_W:(gl + 1) * SG_GROUP_W].astype(F32)
            sg_ref[rows, cs] = (u * mixed).astype(BF16)

    s_in = sg_ref[...]
    for cs, ba, ga_ref, gs_ref in zip(halves, br_a, (ga0_ref, ga1_ref), (gs0_ref, gs1_ref)):
        br_s = jnp.dot(s_in, ws_ref[:, cs], preferred_element_type=F32)
        mix = ga_ref[...].astype(F32) * ba + gs_ref[...].astype(F32) * br_s
        mix_ref[:, cs] = mix.astype(BF16)
    for cs in halves:
        x = jnp.where(pl.program_id(0) < first_tiles, xa_ref[:, cs], xb_ref[:, cs])
        o_ref[:, cs] = x + jnp.dot(mix_ref[...], wo_ref[:, cs], preferred_element_type=F32)


def _mix(att, y, lg, lb, wsp, bs_t, xa, xb, first_rows, wa, ws, wo, layer):
    t = att.shape[0]
    xa_spec, xb_spec = _split_rows(MIX_TM, first_rows)
    row = lambda w: pl.BlockSpec((MIX_TM, w), lambda i: (i, 0))
    half = lambda off: pl.BlockSpec((MIX_TM, HALF), lambda i: (i, off // HALF))
    vec = pl.BlockSpec((1, SG_WIDTH), lambda i: (0, 0))
    weight = _resident((None, D_MODEL, D_MODEL), lambda i: (layer, 0, 0))
    return pl.pallas_call(
        functools.partial(_mix_kernel, first_tiles=first_rows // MIX_TM),
        grid=(t // MIX_TM,),
        in_specs=[
            row(Q_COLS),
            half(U_OFF), half(U_OFF + HALF), half(VS_OFF), half(VS_OFF + HALF),
            vec, vec,
            pl.BlockSpec((SG_GROUPS, SG_CHUNK, SG_CHUNK), lambda i: (0, 0, 0)),
            pl.BlockSpec((SG_CHUNK, SG_GROUPS), lambda i: (0, 0)),
            half(GA_OFF), half(GA_OFF + HALF), half(GS_OFF), half(GS_OFF + HALF),
            xa_spec, xb_spec,
            weight, weight, weight,
        ],
        out_specs=row(D_MODEL),
        out_shape=jax.ShapeDtypeStruct((t, D_MODEL), F32),
        scratch_shapes=[pltpu.VMEM((MIX_TM, SG_WIDTH), BF16), pltpu.VMEM((MIX_TM, D_MODEL), BF16)],
        compiler_params=_params("parallel"),
        name="branch_mix",
    )(att, y, y, y, y, lg, lb, wsp, bs_t, y, y, y, y, xa, xb, wa, ws, wo)


def _mem_fold_kernel(k_ref, v_ref, wq_ref, wo_ref, qk_ref, vo_ref):
    n_mem = k_ref.shape[0]
    scale = X_HEAD_DIM ** -0.5
    for hd in range(X_HEADS):
        cs = slice(hd * X_HEAD_DIM, (hd + 1) * X_HEAD_DIM)
        ms = slice(hd * n_mem, (hd + 1) * n_mem)
        qk = lax.dot_general(wq_ref[:, cs], k_ref[:, cs], (((1,), (1,)), ((), ())),
                             preferred_element_type=F32)
        qk_ref[:, ms] = (qk * scale).astype(BF16)
        vo_ref[ms, :] = jnp.dot(v_ref[:, cs], wo_ref[cs, :],
                                preferred_element_type=F32).astype(BF16)


def _mem_fold(kv, wq, wo, layer, n_mem):
    n_seq = kv.shape[0] // n_mem
    weight = _resident((None, D_MODEL, D_MODEL), lambda b: (layer, 0, 0))
    return pl.pallas_call(
        _mem_fold_kernel,
        grid=(n_seq,),
        in_specs=[
            pl.BlockSpec((n_mem, D_MODEL), lambda b: (b, 0)),
            pl.BlockSpec((n_mem, D_MODEL), lambda b: (b, 1)),
            weight, weight,
        ],
        out_specs=[
            pl.BlockSpec((None, D_MODEL, X_HEADS * n_mem), lambda b: (b, 0, 0)),
            pl.BlockSpec((None, X_HEADS * n_mem, D_MODEL), lambda b: (b, 0, 0)),
        ],
        out_shape=[
            jax.ShapeDtypeStruct((n_seq, D_MODEL, X_HEADS * n_mem), BF16),
            jax.ShapeDtypeStruct((n_seq, X_HEADS * n_mem, D_MODEL), BF16),
        ],
        compiler_params=_params("parallel"),
        name="memory_fold",
    )(kv, kv, wq, wo)


def _xattn_kernel(x_ref, g_ref, qk_ref, vo_ref, o_ref, p_ref):
    n_mem = qk_ref.shape[1] // X_HEADS
    h = _rms(x_ref[...], g_ref[...]).astype(BF16)
    s = jnp.dot(h, qk_ref[...], preferred_element_type=F32)
    for hd in range(X_HEADS):
        ms = slice(hd * n_mem, (hd + 1) * n_mem)
        e = jnp.exp(s[:, ms] - jnp.max(s[:, ms], axis=-1, keepdims=True))
        p_ref[:, ms] = (e / jnp.sum(e, axis=-1, keepdims=True)).astype(BF16)
    for n in range(D_MODEL // HALF):
        cs = slice(n * HALF, (n + 1) * HALF)
        o_ref[:, cs] = x_ref[:, cs] + jnp.dot(p_ref[...], vo_ref[:, cs],
                                              preferred_element_type=F32)


def _xattn(x, g, qk, vo, seq):
    t = x.shape[0]
    tiles_per_seq = seq // XA_TM
    row = pl.BlockSpec((XA_TM, D_MODEL), lambda i: (i, 0))
    return pl.pallas_call(
        _xattn_kernel,
        grid=(t // XA_TM,),
        in_specs=[
            row,
            pl.BlockSpec((1, D_MODEL), lambda i: (0, 0)),
            pl.BlockSpec((None,) + qk.shape[1:], lambda i: (i // tiles_per_seq, 0, 0)),
            pl.BlockSpec((None,) + vo.shape[1:], lambda i: (i // tiles_per_seq, 0, 0)),
        ],
        out_specs=row,
        out_shape=jax.ShapeDtypeStruct((t, D_MODEL), F32),
        scratch_shapes=[pltpu.VMEM((XA_TM, qk.shape[2]), BF16)],
        compiler_params=_params("parallel"),
        name="memory_attention",
    )(x, g, qk, vo)


def _ffn_kernel(x_ref, xp_ref, xn_ref, g_ref, wa_ref, wb_ref, cw_ref, cb_ref, wd_ref, fg_ref,
                *refs, tiles_per_seq, split_tiles):
    if split_tiles is None:
        o_ref, h_ref = refs
        acc_ref = o_ref
    else:
        oa_ref, ob_ref, h_ref, acc_ref = refs
    i = pl.program_id(0)
    f = pl.program_id(1)
    tm = x_ref.shape[0]
    ext = tm + 2 * HALO

    @pl.when(f == 0)
    def _():
        pos = i % tiles_per_seq
        hp = _rms(xp_ref[...], g_ref[...])
        hn = _rms(xn_ref[...], g_ref[...])
        h_ref[:tm, :] = _rms(x_ref[...], g_ref[...]).astype(BF16)
        h_ref[tm:tm + HALO, :] = jnp.where(pos != tiles_per_seq - 1, hn, 0.0).astype(BF16)
        h_ref[tm + HALO:, :] = jnp.where(pos != 0, hp, 0.0).astype(BF16)
        acc_ref[...] = x_ref[...]

    a_ext = jnp.dot(h_ref[...], wa_ref[...], preferred_element_type=F32)
    b = jnp.dot(h_ref[:tm, :], wb_ref[...], preferred_element_type=F32)
    a_prev = pltpu.roll(a_ext, 1, 0)[:tm]
    a_next = pltpu.roll(a_ext, ext - 1, 0)[:tm]
    conv = (cb_ref[...] + a_prev * cw_ref[0:1, :] + a_ext[:tm] * cw_ref[1:2, :]
            + a_next * cw_ref[2:3, :])
    y = (_gelu_tanh(conv) * b).astype(BF16)
    acc_ref[...] += jnp.dot(y, wd_ref[...], preferred_element_type=F32)

    if split_tiles is not None:
        last = f == pl.num_programs(1) - 1

        @pl.when(last & (i < split_tiles))
        def _():
            oa_ref[...] = _rms(acc_ref[...], fg_ref[...])

        @pl.when(last & (i >= split_tiles))
        def _():
            ob_ref[...] = _rms(acc_ref[...], fg_ref[...])


def _ffn(x, g, w_up, conv_w, conv_b, w_down, fg, layer, seq, split_rows):
    t = x.shape[0]
    nf = D_FF // FFN_TF
    tiles_per_seq = seq // FFN_TM
    halo_blocks = FFN_TM // HALO
    last_halo = t // HALO - 1
    row = pl.BlockSpec((FFN_TM, D_MODEL), lambda i, f: (i, 0))
    vec = pl.BlockSpec((1, D_MODEL), lambda i, f: (0, 0))
    scratch = [pltpu.VMEM((FFN_TM + 2 * HALO, D_MODEL), BF16)]
    if split_rows is None:
        split_tiles = None
        out_specs = row
        out_shape = jax.ShapeDtypeStruct((t, D_MODEL), F32)
    else:
        split_tiles = split_rows // FFN_TM
        out_specs = list(_split_rows(FFN_TM, split_rows))
        out_shape = [jax.ShapeDtypeStruct((split_rows, D_MODEL), F32),
                     jax.ShapeDtypeStruct((t - split_rows, D_MODEL), F32)]
        scratch.append(pltpu.VMEM((FFN_TM, D_MODEL), F32))
    kern = functools.partial(_ffn_kernel, tiles_per_seq=tiles_per_seq, split_tiles=split_tiles)
    return pl.pallas_call(
        kern,
        grid=(t // FFN_TM, nf),
        in_specs=[
            row,
            pl.BlockSpec((HALO, D_MODEL), lambda i, f: (jnp.maximum(i * halo_blocks - 1, 0), 0)),
            pl.BlockSpec((HALO, D_MODEL),
                         lambda i, f: (jnp.minimum((i + 1) * halo_blocks, last_halo), 0)),
            vec,
            pl.BlockSpec((None, D_MODEL, FFN_TF), lambda i, f: (layer, 0, f)),
            pl.BlockSpec((None, D_MODEL, FFN_TF), lambda i, f: (layer, 0, nf + f)),
            pl.BlockSpec((3, FFN_TF), lambda i, f: (0, f)),
            pl.BlockSpec((1, FFN_TF), lambda i, f: (0, f)),
            pl.BlockSpec((None, FFN_TF, D_MODEL), lambda i, f: (layer, f, 0)),
            vec,
        ],
        out_specs=out_specs,
        out_shape=out_shape,
        scratch_shapes=scratch,
        compiler_params=_params("parallel" if split_rows is None else "arbitrary", "arbitrary"),
        name="conv_ffn",
    )(x, x, x, g, w_up, w_up, conv_w, conv_b, w_down, fg)


def _rope_tables(seq):
    t = jnp.arange(seq)
    row = (t // GRID_W).astype(F32)
    col = (t % GRID_W).astype(F32)
    quarter = HEAD_DIM // 4
    inv = ROPE_BASE ** (-jnp.arange(quarter, dtype=F32) / quarter)
    ang_r = row[:, None] * inv[None, :]
    ang_c = col[:, None] * inv[None, :]
    ang = jnp.concatenate([ang_r, ang_r, ang_c, ang_c], axis=-1)
    cos, sin = jnp.cos(ang), jnp.sin(ang)
    first = (jnp.arange(HEAD_DIM) % (2 * quarter)) < quarter
    return cos, jnp.where(first, -sin, 0.0), jnp.where(first, 0.0, sin)


def kernel(x_prompt, x_sample, mem_prompt, mem_sample, norm_mix_g, w_in, q_norm_g, k_norm_g, w_attn_o, sg_norm_g, sg_norm_b, w_spatial, b_spatial, w_sg_o, w_out, norm_x_g, norm_mem_g, w_xq, w_xkv, w_xo, norm_ffn_g, w_ffn_up, conv_w, conv_b, w_ffn_down, final_norm_g):
    depth = w_in.shape[0]
    seq = x_prompt.shape[1]
    assert x_sample.shape[1] == seq and seq % IN_TM == 0 and seq % GRID_W == 0
    assert w_in.shape[1:] == (D_MODEL, N_IN) and w_ffn_up.shape[2] == 2 * D_FF
    n_prompt = x_prompt.shape[0] * seq

    xa, xb, first_rows = x_prompt.reshape(-1, D_MODEL), x_sample.reshape(-1, D_MODEL), n_prompt
    total_rows = n_prompt + x_sample.shape[0] * seq
    mem = jnp.concatenate([mem_prompt.reshape(-1, D_MODEL), mem_sample.reshape(-1, D_MODEL)], axis=0)
    n_mem = mem_prompt.shape[1]
    assert mem_sample.shape[1] == n_mem

    cos, sina, sinb = _rope_tables(seq)
    row = lambda v: v.reshape(1, -1).astype(F32)
    bf = lambda w: w.astype(BF16)
    w_in_b, w_attn_o_b, w_sg_o_b, w_out_b = bf(w_in), bf(w_attn_o), bf(w_sg_o), bf(w_out)
    w_xq_b, w_xkv_b, w_xo_b = bf(w_xq), bf(w_xkv), bf(w_xo)
    w_up_b, w_down_b, w_sp_b = bf(w_ffn_up), bf(w_ffn_down), bf(w_spatial)
    q_scale = HEAD_DIM ** -0.5 * LOG2_E
    fg = row(final_norm_g)
    mem_tm = MEM_TM_MAX
    while mem.shape[0] % mem_tm:
        mem_tm //= 2
    assert mem_tm % SUBLANES == 0

    for l in range(depth):
        y = None
        for part, off in ((xa, 0), (xb, first_rows)) if xb is not xa else ((xa, 0),):
            y = _in_proj(part, total_rows, off, y, row(norm_mix_g[l]), w_in_b, l, cos, sina, sinb,
                         row(q_norm_g[l]) * q_scale, row(k_norm_g[l]), seq)
        att = _attention(y, seq)
        x = _mix(att, y, row(sg_norm_g[l]), row(sg_norm_b[l]), w_sp_b[l], b_spatial[l].T,
                 xa, xb, first_rows, w_attn_o_b, w_sg_o_b, w_out_b, l)
        kv = _rms_matmul(mem, row(norm_mem_g[l]), w_xkv_b, l, mem_tm, MEM_TN)
        qk, vo = _mem_fold(kv, w_xq_b, w_xo_b, l, n_mem)
        x = _xattn(x, row(norm_x_g[l]), qk, vo, seq)
        x = _ffn(x, row(norm_ffn_g[l]), w_up_b, conv_w[l], row(conv_b[l]), w_down_b, fg,
                 l, seq, n_prompt if l == depth - 1 else None)
        xa, xb, first_rows = x, x, total_rows

    y_prompt, y_sample = x
    return (y_prompt.reshape(x_prompt.shape), y_sample.reshape(x_sample.shape))
```

```python
import functools

import jax
import jax.numpy as jnp
from jax import lax
from jax.experimental import pallas as pl
from jax.experimental.pallas import tpu as pltpu

F32 = jnp.float32
BF16 = jnp.bfloat16

D_MODEL = 2048
GRID_W = 64
HEAD_DIM = 128
N_HEADS = D_MODEL // HEAD_DIM
N_KV_HEADS = N_HEADS // 4
GROUP = N_HEADS // N_KV_HEADS
ROPE_BASE = 10000.0
SG_WIDTH = D_MODEL
SG_GROUPS = 8
SG_CHUNK = 128
SG_GROUP_W = SG_WIDTH // SG_GROUPS
X_HEADS = 4
X_HEAD_DIM = D_MODEL // X_HEADS
D_FF = 5632
EPS = 1e-6
LOG2_E = 1.4426950408889634
GELU_C = 0.7978845608028654

Q_COLS = N_HEADS * HEAD_DIM
KV_COLS = N_KV_HEADS * HEAD_DIM
K_OFF = Q_COLS
V_OFF = K_OFF + KV_COLS
U_OFF = V_OFF + KV_COLS
VS_OFF = U_OFF + SG_WIDTH
GA_OFF = VS_OFF + SG_WIDTH
GS_OFF = GA_OFF + D_MODEL
N_IN = GS_OFF + D_MODEL

LANES = 128
SUBLANES = 8
VMEM_LIMIT_BYTES = 56 * 1024 * 1024

IN_TM = 1024
IN_TN = 1024
IN_SUB = 256
ATT_TQ = 1024
ATT_SUB = 256
MIX_TM = 256
HALF = 1024
XA_TM = 512
FFN_TM = 512
FFN_TF = 512
MEM_TM_MAX = 1024
MEM_TN = 512
HALO = SUBLANES


def _params(*sem):
    return pltpu.CompilerParams(dimension_semantics=sem, vmem_limit_bytes=VMEM_LIMIT_BYTES)


def _rms(xf, g):
    ms = jnp.mean(xf * xf, axis=-1, keepdims=True)
    return xf * lax.rsqrt(ms + EPS) * g


def _gelu_tanh(x):
    inner = x * (GELU_C + (GELU_C * 0.044715) * (x * x))
    return (0.5 * x) * (1.0 + jnp.tanh(inner))


def _split_rows(tm, first_rows):
    assert first_rows % tm == 0
    nf = first_rows // tm
    first = pl.BlockSpec((tm, D_MODEL), lambda i, *_: (jnp.minimum(i, nf - 1), 0))
    second = pl.BlockSpec((tm, D_MODEL), lambda i, *_: (jnp.maximum(i - nf, 0), 0))
    return first, second


def _resident(shape, index_map):
    return pl.BlockSpec(shape, index_map, pipeline_mode=pl.Buffered(1))


def _in_proj_kernel(x_ref, g_ref, w_ref, cos_ref, sina_ref, sinb_ref, qg_ref, kg_ref,
                    o_ref, h_ref):
    j = pl.program_id(1)
    tm = x_ref.shape[0]

    @pl.when(j == 0)
    def _():
        for r in range(0, tm, IN_SUB):
            h_ref[r:r + IN_SUB, :] = _rms(x_ref[r:r + IN_SUB, :], g_ref[...]).astype(BF16)

    def norm_rope(y, rows, n_heads, gain_ref):
        for hh in range(n_heads):
            cs = slice(hh * HEAD_DIM, (hh + 1) * HEAD_DIM)
            yn = _rms(y[:, cs], gain_ref[...])
            out = (yn * cos_ref[rows, :]
                   + pltpu.roll(yn, HEAD_DIM - HEAD_DIM // 4, 1) * sina_ref[rows, :]
                   + pltpu.roll(yn, HEAD_DIM // 4, 1) * sinb_ref[rows, :])
            o_ref[rows, cs] = out.astype(BF16)

    def run(epilogue):
        for r in range(0, tm, IN_SUB):
            rows = slice(r, r + IN_SUB)
            y = jnp.dot(h_ref[rows, :], w_ref[...], preferred_element_type=F32)
            epilogue(y, rows)

    def store(fn):
        def ep(y, rows):
            o_ref[rows, :] = fn(y).astype(BF16)
        return ep

    @pl.when(j < K_OFF // IN_TN)
    def _():
        run(lambda y, rows: norm_rope(y, rows, IN_TN // HEAD_DIM, qg_ref))

    @pl.when(j == K_OFF // IN_TN)
    def _():
        def ep(y, rows):
            norm_rope(y, rows, N_KV_HEADS, kg_ref)
            o_ref[rows, KV_COLS:] = y[:, KV_COLS:].astype(BF16)
        run(ep)

    @pl.when((j >= U_OFF // IN_TN) & (j < GA_OFF // IN_TN))
    def _():
        run(store(_gelu_tanh))

    @pl.when(j >= GA_OFF // IN_TN)
    def _():
        run(store(jax.nn.sigmoid))


def _in_proj(x, g, w_in, layer, cos, sina, sinb, qg, kg, seq):
    t = x.shape[0]
    tiles_per_seq = seq // IN_TM
    rope_spec = pl.BlockSpec((IN_TM, HEAD_DIM), lambda i, j: (i % tiles_per_seq, 0))
    vec = lambda n: pl.BlockSpec((1, n), lambda i, j: (0, 0))
    return pl.pallas_call(
        _in_proj_kernel,
        grid=(t // IN_TM, N_IN // IN_TN),
        in_specs=[
            pl.BlockSpec((IN_TM, D_MODEL), lambda i, j: (i, 0)),
            vec(D_MODEL),
            pl.BlockSpec((None, D_MODEL, IN_TN), lambda i, j: (layer, 0, j)),
            rope_spec, rope_spec, rope_spec,
            vec(HEAD_DIM), vec(HEAD_DIM),
        ],
        out_specs=pl.BlockSpec((IN_TM, IN_TN), lambda i, j: (i, j)),
        out_shape=jax.ShapeDtypeStruct((t, N_IN), BF16),
        scratch_shapes=[pltpu.VMEM((IN_TM, D_MODEL), BF16)],
        compiler_params=_params("parallel", "arbitrary"),
        name="in_proj",
    )(x, g, w_in, cos, sina, sinb, qg, kg)


def _rms_matmul_kernel(x_ref, g_ref, w_ref, o_ref, h_ref):
    @pl.when(pl.program_id(1) == 0)
    def _():
        h_ref[...] = _rms(x_ref[...], g_ref[...]).astype(BF16)

    o_ref[...] = jnp.dot(h_ref[...], w_ref[...], preferred_element_type=F32).astype(BF16)


def _rms_matmul(x, g, w, layer, tm, tn):
    t = x.shape[0]
    n = w.shape[-1]
    return pl.pallas_call(
        _rms_matmul_kernel,
        grid=(t // tm, n // tn),
        in_specs=[
            pl.BlockSpec((tm, D_MODEL), lambda i, j: (i, 0)),
            pl.BlockSpec((1, D_MODEL), lambda i, j: (0, 0)),
            pl.BlockSpec((None, D_MODEL, tn), lambda i, j: (layer, 0, j)),
        ],
        out_specs=pl.BlockSpec((tm, tn), lambda i, j: (i, j)),
        out_shape=jax.ShapeDtypeStruct((t, n), BF16),
        scratch_shapes=[pltpu.VMEM((tm, D_MODEL), BF16)],
        compiler_params=_params("parallel", "arbitrary"),
        name="mem_kv_proj",
    )(x, g, w)


def _attention_kernel(q_ref, k_ref, v_ref, o_ref, kt_ref):
    @pl.when(pl.program_id(2) == 0)
    def _():
        kt_ref[...] = k_ref[...].T

    v = v_ref[...]
    blocks = [(g, r) for g in range(GROUP) for r in range(0, ATT_TQ, ATT_SUB)]

    def scores(g, r):
        q = q_ref[r:r + ATT_SUB, g * HEAD_DIM:(g + 1) * HEAD_DIM]
        return jnp.dot(q, kt_ref[...], preferred_element_type=F32)

    s = scores(*blocks[0])
    for n, (g, r) in enumerate(blocks):
        s_next = scores(*blocks[n + 1]) if n + 1 < len(blocks) else None
        p = jnp.exp2(s - jnp.max(s, axis=-1, keepdims=True))
        l = jnp.sum(p, axis=-1, keepdims=True)
        o = jnp.dot(p.astype(BF16), v, preferred_element_type=F32)
        o_ref[r:r + ATT_SUB, g * HEAD_DIM:(g + 1) * HEAD_DIM] = (o / l).astype(BF16)
        s = s_next


def _attention(y, seq):
    t = y.shape[0]
    n_seq = t // seq
    q_tiles = seq // ATT_TQ
    gw = GROUP * HEAD_DIM
    return pl.pallas_call(
        _attention_kernel,
        grid=(n_seq, N_KV_HEADS, q_tiles),
        in_specs=[
            pl.BlockSpec((ATT_TQ, gw), lambda b, h, qi: (b * q_tiles + qi, h)),
            pl.BlockSpec((seq, HEAD_DIM), lambda b, h, qi: (b, K_OFF // HEAD_DIM + h)),
            pl.BlockSpec((seq, HEAD_DIM), lambda b, h, qi: (b, V_OFF // HEAD_DIM + h)),
        ],
        out_specs=pl.BlockSpec((ATT_TQ, gw), lambda b, h, qi: (b * q_tiles + qi, h)),
        out_shape=jax.ShapeDtypeStruct((t, Q_COLS), BF16),
        scratch_shapes=[pltpu.VMEM((HEAD_DIM, seq), BF16)],
        compiler_params=_params("parallel", "parallel", "arbitrary"),
        name="attention",
    )(y, y, y)


def _mix_kernel(a_ref, u0_ref, u1_ref, v0_ref, v1_ref, lg_ref, lb_ref, wsp_ref, bs_ref,
                ga0_ref, ga1_ref, gs0_ref, gs1_ref, x_ref,
                wa_ref, ws_ref, wo_ref, o_ref, sg_ref, mix_ref):
    halves = [slice(n * HALF, (n + 1) * HALF) for n in range(D_MODEL // HALF)]
    a_in = a_ref[...]
    br_a = [jnp.dot(a_in, wa_ref[:, cs], preferred_element_type=F32) for cs in halves]

    groups_per_half = HALF // SG_GROUP_W
    for c in range(MIX_TM // SG_CHUNK):
        rows = slice(c * SG_CHUNK, (c + 1) * SG_CHUNK)
        v = jnp.concatenate([v0_ref[rows, :], v1_ref[rows, :]], axis=-1).astype(F32)
        xc = v - jnp.mean(v, axis=-1, keepdims=True)
        var = jnp.mean(xc * xc, axis=-1, keepdims=True)
        vn = (xc * lax.rsqrt(var + EPS) * lg_ref[...] + lb_ref[...]).astype(BF16)
        for g in range(SG_GROUPS):
            cs = slice(g * SG_GROUP_W, (g + 1) * SG_GROUP_W)
            mixed = jnp.dot(wsp_ref[g], vn[:, cs], preferred_element_type=F32) + bs_ref[:, g:g + 1]
            u_ref = u0_ref if g < groups_per_half else u1_ref
            gl = g % groups_per_half
            u = u_ref[rows, gl * SG_GROUP_W:(gl + 1) * SG_GROUP_W].astype(F32)
            sg_ref[rows, cs] = (u * mixed).astype(BF16)

    s_in = sg_ref[...]
    for cs, ba, ga_ref, gs_ref in zip(halves, br_a, (ga0_ref, ga1_ref), (gs0_ref, gs1_ref)):
        br_s = jnp.dot(s_in, ws_ref[:, cs], preferred_element_type=F32)
        mix = ga_ref[...].astype(F32) * ba + gs_ref[...].astype(F32) * br_s
        mix_ref[:, cs] = mix.astype(BF16)
    for cs in halves:
        o_ref[:, cs] = x_ref[:, cs] + jnp.dot(mix_ref[...], wo_ref[:, cs],
                                              preferred_element_type=F32)


def _mix(att, y, lg, lb, wsp, bs_t, x, wa, ws, wo, layer):
    t = att.shape[0]
    row = lambda w: pl.BlockSpec((MIX_TM, w), lambda i: (i, 0))
    half = lambda off: pl.BlockSpec((MIX_TM, HALF), lambda i: (i, off // HALF))
    vec = pl.BlockSpec((1, SG_WIDTH), lambda i: (0, 0))
    weight = _resident((None, D_MODEL, D_MODEL), lambda i: (layer, 0, 0))
    return pl.pallas_call(
        _mix_kernel,
        grid=(t // MIX_TM,),
        in_specs=[
            row(Q_COLS),
            half(U_OFF), half(U_OFF + HALF), half(VS_OFF), half(VS_OFF + HALF),
            vec, vec,
            pl.BlockSpec((SG_GROUPS, SG_CHUNK, SG_CHUNK), lambda i: (0, 0, 0)),
            pl.BlockSpec((SG_CHUNK, SG_GROUPS), lambda i: (0, 0)),
            half(GA_OFF), half(GA_OFF + HALF), half(GS_OFF), half(GS_OFF + HALF),
            row(D_MODEL),
            weight, weight, weight,
        ],
        out_specs=row(D_MODEL),
        out_shape=jax.ShapeDtypeStruct((t, D_MODEL), F32),
        scratch_shapes=[pltpu.VMEM((MIX_TM, SG_WIDTH), BF16), pltpu.VMEM((MIX_TM, D_MODEL), BF16)],
        compiler_params=_params("parallel"),
        name="branch_mix",
    )(att, y, y, y, y, lg, lb, wsp, bs_t, y, y, y, y, x, wa, ws, wo)


def _mem_fold_kernel(k_ref, v_ref, wq_ref, wo_ref, qk_ref, vo_ref):
    n_mem = k_ref.shape[0]
    scale = X_HEAD_DIM ** -0.5
    for hd in range(X_HEADS):
        cs = slice(hd * X_HEAD_DIM, (hd + 1) * X_HEAD_DIM)
        ms = slice(hd * n_mem, (hd + 1) * n_mem)
        qk = lax.dot_general(wq_ref[:, cs], k_ref[:, cs], (((1,), (1,)), ((), ())),
                             preferred_element_type=F32)
        qk_ref[:, ms] = (qk * scale).astype(BF16)
        vo_ref[ms, :] = jnp.dot(v_ref[:, cs], wo_ref[cs, :],
                                preferred_element_type=F32).astype(BF16)


def _mem_fold(kv, wq, wo, layer, n_mem):
    n_seq = kv.shape[0] // n_mem
    weight = _resident((None, D_MODEL, D_MODEL), lambda b: (layer, 0, 0))
    return pl.pallas_call(
        _mem_fold_kernel,
        grid=(n_seq,),
        in_specs=[
            pl.BlockSpec((n_mem, D_MODEL), lambda b: (b, 0)),
            pl.BlockSpec((n_mem, D_MODEL), lambda b: (b, 1)),
            weight, weight,
        ],
        out_specs=[
            pl.BlockSpec((None, D_MODEL, X_HEADS * n_mem), lambda b: (b, 0, 0)),
            pl.BlockSpec((None, X_HEADS * n_mem, D_MODEL), lambda b: (b, 0, 0)),
        ],
        out_shape=[
            jax.ShapeDtypeStruct((n_seq, D_MODEL, X_HEADS * n_mem), BF16),
            jax.ShapeDtypeStruct((n_seq, X_HEADS * n_mem, D_MODEL), BF16),
        ],
        compiler_params=_params("parallel"),
        name="memory_fold",
    )(kv, kv, wq, wo)


def _xattn_kernel(x_ref, g_ref, qk_ref, vo_ref, o_ref, p_ref):
    n_mem = qk_ref.shape[1] // X_HEADS
    h = _rms(x_ref[...], g_ref[...]).astype(BF16)
    s = jnp.dot(h, qk_ref[...], preferred_element_type=F32)
    for hd in range(X_HEADS):
        ms = slice(hd * n_mem, (hd + 1) * n_mem)
        e = jnp.exp(s[:, ms] - jnp.max(s[:, ms], axis=-1, keepdims=True))
        p_ref[:, ms] = (e / jnp.sum(e, axis=-1, keepdims=True)).astype(BF16)
    for n in range(D_MODEL // HALF):
        cs = slice(n * HALF, (n + 1) * HALF)
        o_ref[:, cs] = x_ref[:, cs] + jnp.dot(p_ref[...], vo_ref[:, cs],
                                              preferred_element_type=F32)


def _xattn(x, g, qk, vo, seq):
    t = x.shape[0]
    tiles_per_seq = seq // XA_TM
    row = pl.BlockSpec((XA_TM, D_MODEL), lambda i: (i, 0))
    return pl.pallas_call(
        _xattn_kernel,
        grid=(t // XA_TM,),
        in_specs=[
            row,
            pl.BlockSpec((1, D_MODEL), lambda i: (0, 0)),
            pl.BlockSpec((None,) + qk.shape[1:], lambda i: (i // tiles_per_seq, 0, 0)),
            pl.BlockSpec((None,) + vo.shape[1:], lambda i: (i // tiles_per_seq, 0, 0)),
        ],
        out_specs=row,
        out_shape=jax.ShapeDtypeStruct((t, D_MODEL), F32),
        scratch_shapes=[pltpu.VMEM((XA_TM, qk.shape[2]), BF16)],
        compiler_params=_params("parallel"),
        name="memory_attention",
    )(x, g, qk, vo)


def _ffn_kernel(x_ref, xp_ref, xn_ref, g_ref, wa_ref, wb_ref, cw_ref, cb_ref, wd_ref, fg_ref,
                *refs, tiles_per_seq, split_tiles):
    if split_tiles is None:
        o_ref, h_ref = refs
        acc_ref = o_ref
    else:
        oa_ref, ob_ref, h_ref, acc_ref = refs
    i = pl.program_id(0)
    f = pl.program_id(1)
    tm = x_ref.shape[0]
    ext = tm + 2 * HALO

    @pl.when(f == 0)
    def _():
        pos = i % tiles_per_seq
        hp = _rms(xp_ref[...], g_ref[...])
        hn = _rms(xn_ref[...], g_ref[...])
        h_ref[:tm, :] = _rms(x_ref[...], g_ref[...]).astype(BF16)
        h_ref[tm:tm + HALO, :] = jnp.where(pos != tiles_per_seq - 1, hn, 0.0).astype(BF16)
        h_ref[tm + HALO:, :] = jnp.where(pos != 0, hp, 0.0).astype(BF16)
        acc_ref[...] = x_ref[...]

    a_ext = jnp.dot(h_ref[...], wa_ref[...], preferred_element_type=F32)
    b = jnp.dot(h_ref[:tm, :], wb_ref[...], preferred_element_type=F32)
    a_prev = pltpu.roll(a_ext, 1, 0)[:tm]
    a_next = pltpu.roll(a_ext, ext - 1, 0)[:tm]
    conv = (cb_ref[...] + a_prev * cw_ref[0:1, :] + a_ext[:tm] * cw_ref[1:2, :]
            + a_next * cw_ref[2:3, :])
    y = (_gelu_tanh(conv) * b).astype(BF16)
    acc_ref[...] += jnp.dot(y, wd_ref[...], preferred_element_type=F32)

    if split_tiles is not None:
        last = f == pl.num_programs(1) - 1

        @pl.when(last & (i < split_tiles))
        def _():
            oa_ref[...] = _rms(acc_ref[...], fg_ref[...])

        @pl.when(last & (i >= split_tiles))
        def _():
            ob_ref[...] = _rms(acc_ref[...], fg_ref[...])


def _ffn(x, g, w_up, conv_w, conv_b, w_down, fg, layer, seq, split_rows):
    t = x.shape[0]
    nf = D_FF // FFN_TF
    tiles_per_seq = seq // FFN_TM
    halo_blocks = FFN_TM // HALO
    last_halo = t // HALO - 1
    row = pl.BlockSpec((FFN_TM, D_MODEL), lambda i, f: (i, 0))
    vec = pl.BlockSpec((1, D_MODEL), lambda i, f: (0, 0))
    scratch = [pltpu.VMEM((FFN_TM + 2 * HALO, D_MODEL), BF16)]
    if split_rows is None:
        split_tiles = None
        out_specs = row
        out_shape = jax.ShapeDtypeStruct((t, D_MODEL), F32)
    else:
        split_tiles = split_rows // FFN_TM
        out_specs = list(_split_rows(FFN_TM, split_rows))
        out_shape = [jax.ShapeDtypeStruct((split_rows, D_MODEL), F32),
                     jax.ShapeDtypeStruct((t - split_rows, D_MODEL), F32)]
        scratch.append(pltpu.VMEM((FFN_TM, D_MODEL), F32))
    kern = functools.partial(_ffn_kernel, tiles_per_seq=tiles_per_seq, split_tiles=split_tiles)
    return pl.pallas_call(
        kern,
        grid=(t // FFN_TM, nf),
        in_specs=[
            row,
            pl.BlockSpec((HALO, D_MODEL), lambda i, f: (jnp.maximum(i * halo_blocks - 1, 0), 0)),
            pl.BlockSpec((HALO, D_MODEL),
                         lambda i, f: (jnp.minimum((i + 1) * halo_blocks, last_halo), 0)),
            vec,
            pl.BlockSpec((None, D_MODEL, FFN_TF), lambda i, f: (layer, 0, f)),
            pl.BlockSpec((None, D_MODEL, FFN_TF), lambda i, f: (layer, 0, nf + f)),
            pl.BlockSpec((3, FFN_TF), lambda i, f: (0, f)),
            pl.BlockSpec((1, FFN_TF), lambda i, f: (0, f)),
            pl.BlockSpec((None, FFN_TF, D_MODEL), lambda i, f: (layer, f, 0)),
            vec,
        ],
        out_specs=out_specs,
        out_shape=out_shape,
        scratch_shapes=scratch,
        compiler_params=_params("parallel" if split_rows is None else "arbitrary", "arbitrary"),
        name="conv_ffn",
    )(x, x, x, g, w_up, w_up, conv_w, conv_b, w_down, fg)


def _rope_tables(seq):
    t = jnp.arange(seq)
    row = (t // GRID_W).astype(F32)
    col = (t % GRID_W).astype(F32)
    quarter = HEAD_DIM // 4
    inv = ROPE_BASE ** (-jnp.arange(quarter, dtype=F32) / quarter)
    ang_r = row[:, None] * inv[None, :]
    ang_c = col[:, None] * inv[None, :]
    ang = jnp.concatenate([ang_r, ang_r, ang_c, ang_c], axis=-1)
    cos, sin = jnp.cos(ang), jnp.sin(ang)
    first = (jnp.arange(HEAD_DIM) % (2 * quarter)) < quarter
    return cos, jnp.where(first, -sin, 0.0), jnp.where(first, 0.0, sin)


def kernel(x_prompt, x_sample, mem_prompt, mem_sample, norm_mix_g, w_in, q_norm_g, k_norm_g, w_attn_o, sg_norm_g, sg_norm_b, w_spatial, b_spatial, w_sg_o, w_out, norm_x_g, norm_mem_g, w_xq, w_xkv, w_xo, norm_ffn_g, w_ffn_up, conv_w, conv_b, w_ffn_down, final_norm_g):
    depth = w_in.shape[0]
    seq = x_prompt.shape[1]
    assert x_sample.shape[1] == seq and seq % IN_TM == 0 and seq % GRID_W == 0
    assert w_in.shape[1:] == (D_MODEL, N_IN) and w_ffn_up.shape[2] == 2 * D_FF
    n_prompt = x_prompt.shape[0] * seq

    x = jnp.concatenate([x_prompt.reshape(-1, D_MODEL), x_sample.reshape(-1, D_MODEL)], axis=0)
    mem = jnp.concatenate([mem_prompt.reshape(-1, D_MODEL), mem_sample.reshape(-1, D_MODEL)], axis=0)
    n_mem = mem_prompt.shape[1]
    assert mem_sample.shape[1] == n_mem

    cos, sina, sinb = _rope_tables(seq)
    row = lambda v: v.reshape(1, -1).astype(F32)
    bf = lambda w: w.astype(BF16)
    w_in_b, w_attn_o_b, w_sg_o_b, w_out_b = bf(w_in), bf(w_attn_o), bf(w_sg_o), bf(w_out)
    w_xq_b, w_xkv_b, w_xo_b = bf(w_xq), bf(w_xkv), bf(w_xo)
    w_up_b, w_down_b, w_sp_b = bf(w_ffn_up), bf(w_ffn_down), bf(w_spatial)
    q_scale = HEAD_DIM ** -0.5 * LOG2_E
    fg = row(final_norm_g)
    mem_tm = MEM_TM_MAX
    while mem.shape[0] % mem_tm:
        mem_tm //= 2
    assert mem_tm % SUBLANES == 0

    for l in range(depth):
        y = _in_proj(x, row(norm_mix_g[l]), w_in_b, l, cos, sina, sinb,
                     row(q_norm_g[l]) * q_scale, row(k_norm_g[l]), seq)
        att = _attention(y, seq)
        x = _mix(att, y, row(sg_norm_g[l]), row(sg_norm_b[l]), w_sp_b[l], b_spatial[l].T, x,
                 w_attn_o_b, w_sg_o_b, w_out_b, l)
        kv = _rms_matmul(mem, row(norm_mem_g[l]), w_xkv_b, l, mem_tm, MEM_TN)
        qk, vo = _mem_fold(kv, w_xq_b, w_xo_b, l, n_mem)
        x = _xattn(x, row(norm_x_g[l]), qk, vo, seq)
        x = _ffn(x, row(norm_ffn_g[l]), w_up_b, conv_w[l], row(conv_b[l]), w_down_b, fg,
                 l, seq, n_prompt if l == depth - 1 else None)

    y_prompt, y_sample = x
    return (y_prompt.reshape(x_prompt.shape), y_sample.reshape(x_sample.shape))
```

```python
import functools

import jax
import jax.numpy as jnp
from jax import lax
from jax.experimental import pallas as pl
from jax.experimental.pallas import tpu as pltpu

F32 = jnp.float32
BF16 = jnp.bfloat16

D_MODEL = 2048
GRID_W = 64
HEAD_DIM = 128
N_HEADS = D_MODEL // HEAD_DIM
N_KV_HEADS = N_HEADS // 4
GROUP = N_HEADS // N_KV_HEADS
ROPE_BASE = 10000.0
SG_WIDTH = D_MODEL
SG_GROUPS = 8
SG_CHUNK = 128
SG_GROUP_W = SG_WIDTH // SG_GROUPS
X_HEADS = 4
X_HEAD_DIM = D_MODEL // X_HEADS
D_FF = 5632
EPS = 1e-6
LOG2_E = 1.4426950408889634
GELU_C = 0.7978845608028654

Q_COLS = N_HEADS * HEAD_DIM
KV_COLS = N_KV_HEADS * HEAD_DIM
K_OFF = Q_COLS
V_OFF = K_OFF + KV_COLS
U_OFF = V_OFF + KV_COLS
VS_OFF = U_OFF + SG_WIDTH
GA_OFF = VS_OFF + SG_WIDTH
GS_OFF = GA_OFF + D_MODEL
N_IN = GS_OFF + D_MODEL

LANES = 128
SUBLANES = 8
VMEM_LIMIT_BYTES = 56 * 1024 * 1024

IN_TM = 1024
IN_TN = 1024
IN_SUB = 256
ATT_TQ = 1024
ATT_SUB = 256
MIX_TM = 256
HALF = 1024
XA_TM = 512
FFN_TM = 512
FFN_TF = 512
MEM_TM_MAX = 1024
MEM_TN = 512
HALO = SUBLANES


def _params(*sem):
    return pltpu.CompilerParams(dimension_semantics=sem, vmem_limit_bytes=VMEM_LIMIT_BYTES)


def _rms(xf, g):
    ms = jnp.mean(xf * xf, axis=-1, keepdims=True)
    return xf * lax.rsqrt(ms + EPS) * g


def _gelu_tanh(x):
    inner = x * (GELU_C + (GELU_C * 0.044715) * (x * x))
    return (0.5 * x) * (1.0 + jnp.tanh(inner))


def _split_rows(tm, first_rows):
    assert first_rows % tm == 0
    nf = first_rows // tm
    first = pl.BlockSpec((tm, D_MODEL), lambda i, *_: (jnp.minimum(i, nf - 1), 0))
    second = pl.BlockSpec((tm, D_MODEL), lambda i, *_: (jnp.maximum(i - nf, 0), 0))
    return first, second


def _resident(shape, index_map):
    return pl.BlockSpec(shape, index_map, pipeline_mode=pl.Buffered(1))


def _in_proj_kernel(*refs, first_tiles):
    n_x = 1 if first_tiles is None else 2
    x_refs = refs[:n_x]
    g_ref, w_ref, cos_ref, sina_ref, sinb_ref, qg_ref, kg_ref, o_ref, h_ref = refs[n_x:]
    i = pl.program_id(0)
    j = pl.program_id(1)
    tm = x_refs[0].shape[0]

    @pl.when(j == 0)
    def _():
        for r in range(0, tm, IN_SUB):
            rows = slice(r, r + IN_SUB)
            x = x_refs[0][rows, :]
            if n_x == 2:
                x = jnp.where(i < first_tiles, x, x_refs[1][rows, :])
            h_ref[rows, :] = _rms(x, g_ref[...]).astype(BF16)

    def norm_rope(y, rows, n_heads, gain_ref):
        for hh in range(n_heads):
            cs = slice(hh * HEAD_DIM, (hh + 1) * HEAD_DIM)
            yn = _rms(y[:, cs], gain_ref[...])
            out = (yn * cos_ref[rows, :]
                   + pltpu.roll(yn, HEAD_DIM - HEAD_DIM // 4, 1) * sina_ref[rows, :]
                   + pltpu.roll(yn, HEAD_DIM // 4, 1) * sinb_ref[rows, :])
            o_ref[rows, cs] = out.astype(BF16)

    def run(epilogue):
        for r in range(0, tm, IN_SUB):
            rows = slice(r, r + IN_SUB)
            y = jnp.dot(h_ref[rows, :], w_ref[...], preferred_element_type=F32)
            epilogue(y, rows)

    def store(fn):
        def ep(y, rows):
            o_ref[rows, :] = fn(y).astype(BF16)
        return ep

    @pl.when(j < K_OFF // IN_TN)
    def _():
        run(lambda y, rows: norm_rope(y, rows, IN_TN // HEAD_DIM, qg_ref))

    @pl.when(j == K_OFF // IN_TN)
    def _():
        def ep(y, rows):
            norm_rope(y, rows, N_KV_HEADS, kg_ref)
            o_ref[rows, KV_COLS:] = y[:, KV_COLS:].astype(BF16)
        run(ep)

    @pl.when((j >= U_OFF // IN_TN) & (j < GA_OFF // IN_TN))
    def _():
        run(store(_gelu_tanh))

    @pl.when(j >= GA_OFF // IN_TN)
    def _():
        run(store(jax.nn.sigmoid))


def _in_proj(xa, xb, first_rows, g, w_in, layer, cos, sina, sinb, qg, kg, seq):
    tiles_per_seq = seq // IN_TM
    rope_spec = pl.BlockSpec((IN_TM, HEAD_DIM), lambda i, j: (i % tiles_per_seq, 0))
    vec = lambda n: pl.BlockSpec((1, n), lambda i, j: (0, 0))
    if xb is None:
        t, first_tiles, xs = xa.shape[0], None, [xa]
        x_specs = [pl.BlockSpec((IN_TM, D_MODEL), lambda i, j: (i, 0))]
    else:
        t, first_tiles, xs = first_rows + xb.shape[0], first_rows // IN_TM, [xa, xb]
        x_specs = [pl.BlockSpec(sp.block_shape, sp.index_map, pipeline_mode=pl.Buffered(1))
                   for sp in _split_rows(IN_TM, first_rows)]
    return pl.pallas_call(
        functools.partial(_in_proj_kernel, first_tiles=first_tiles),
        grid=(t // IN_TM, N_IN // IN_TN),
        in_specs=x_specs + [
            vec(D_MODEL),
            pl.BlockSpec((None, D_MODEL, IN_TN), lambda i, j: (layer, 0, j)),
            rope_spec, rope_spec, rope_spec,
            vec(HEAD_DIM), vec(HEAD_DIM),
        ],
        out_specs=pl.BlockSpec((IN_TM, IN_TN), lambda i, j: (i, j)),
        out_shape=jax.ShapeDtypeStruct((t, N_IN), BF16),
        scratch_shapes=[pltpu.VMEM((IN_TM, D_MODEL), BF16)],
        compiler_params=_params("parallel", "arbitrary"),
        name="in_proj",
    )(*xs, g, w_in, cos, sina, sinb, qg, kg)


def _rms_matmul_kernel(x_ref, g_ref, w_ref, o_ref, h_ref):
    @pl.when(pl.program_id(1) == 0)
    def _():
        h_ref[...] = _rms(x_ref[...], g_ref[...]).astype(BF16)

    o_ref[...] = jnp.dot(h_ref[...], w_ref[...], preferred_element_type=F32).astype(BF16)


def _rms_matmul(x, g, w, layer, tm, tn):
    t = x.shape[0]
    n = w.shape[-1]
    return pl.pallas_call(
        _rms_matmul_kernel,
        grid=(t // tm, n // tn),
        in_specs=[
            pl.BlockSpec((tm, D_MODEL), lambda i, j: (i, 0)),
            pl.BlockSpec((1, D_MODEL), lambda i, j: (0, 0)),
            pl.BlockSpec((None, D_MODEL, tn), lambda i, j: (layer, 0, j)),
        ],
        out_specs=pl.BlockSpec((tm, tn), lambda i, j: (i, j)),
        out_shape=jax.ShapeDtypeStruct((t, n), BF16),
        scratch_shapes=[pltpu.VMEM((tm, D_MODEL), BF16)],
        compiler_params=_params("parallel", "arbitrary"),
        name="mem_kv_proj",
    )(x, g, w)


def _attention_kernel(q_ref, k_ref, v_ref, o_ref, kt_ref):
    @pl.when(pl.program_id(2) == 0)
    def _():
        kt_ref[...] = k_ref[...].T

    v = v_ref[...]
    blocks = [(g, r) for g in range(GROUP) for r in range(0, ATT_TQ, ATT_SUB)]

    def scores(g, r):
        q = q_ref[r:r + ATT_SUB, g * HEAD_DIM:(g + 1) * HEAD_DIM]
        return jnp.dot(q, kt_ref[...], preferred_element_type=F32)

    s = scores(*blocks[0])
    for n, (g, r) in enumerate(blocks):
        s_next = scores(*blocks[n + 1]) if n + 1 < len(blocks) else None
        p = jnp.exp2(s - jnp.max(s, axis=-1, keepdims=True))
        l = jnp.sum(p, axis=-1, keepdims=True)
        o = jnp.dot(p.astype(BF16), v, preferred_element_type=F32)
        o_ref[r:r + ATT_SUB, g * HEAD_DIM:(g + 1) * HEAD_DIM] = (o / l).astype(BF16)
        s = s_next


def _attention(y, seq):
    t = y.shape[0]
    n_seq = t // seq
    q_tiles = seq // ATT_TQ
    gw = GROUP * HEAD_DIM
    return pl.pallas_call(
        _attention_kernel,
        grid=(n_seq, N_KV_HEADS, q_tiles),
        in_specs=[
            pl.BlockSpec((ATT_TQ, gw), lambda b, h, qi: (b * q_tiles + qi, h)),
            pl.BlockSpec((seq, HEAD_DIM), lambda b, h, qi: (b, K_OFF // HEAD_DIM + h)),
            pl.BlockSpec((seq, HEAD_DIM), lambda b, h, qi: (b, V_OFF // HEAD_DIM + h)),
        ],
        out_specs=pl.BlockSpec((ATT_TQ, gw), lambda b, h, qi: (b * q_tiles + qi, h)),
        out_shape=jax.ShapeDtypeStruct((t, Q_COLS), BF16),
        scratch_shapes=[pltpu.VMEM((HEAD_DIM, seq), BF16)],
        compiler_params=_params("parallel", "parallel", "arbitrary"),
        name="attention",
    )(y, y, y)


def _mix_kernel(a_ref, u0_ref, u1_ref, v0_ref, v1_ref, lg_ref, lb_ref, wsp_ref, bs_ref,
                ga0_ref, ga1_ref, gs0_ref, gs1_ref, xa_ref, xb_ref,
                wa_ref, ws_ref, wo_ref, o_ref, sg_ref, mix_ref, *, first_tiles):
    halves = [slice(n * HALF, (n + 1) * HALF) for n in range(D_MODEL // HALF)]
    a_in = a_ref[...]
    br_a = [jnp.dot(a_in, wa_ref[:, cs], preferred_element_type=F32) for cs in halves]

    groups_per_half = HALF // SG_GROUP_W
    for c in range(MIX_TM // SG_CHUNK):
        rows = slice(c * SG_CHUNK, (c + 1) * SG_CHUNK)
        v = jnp.concatenate([v0_ref[rows, :], v1_ref[rows, :]], axis=-1).astype(F32)
        xc = v - jnp.mean(v, axis=-1, keepdims=True)
        var = jnp.mean(xc * xc, axis=-1, keepdims=True)
        vn = (xc * lax.rsqrt(var + EPS) * lg_ref[...] + lb_ref[...]).astype(BF16)
        for g in range(SG_GROUPS):
            cs = slice(g * SG_GROUP_W, (g + 1) * SG_GROUP_W)
            mixed = jnp.dot(wsp_ref[g], vn[:, cs], preferred_element_type=F32) + bs_ref[:, g:g + 1]
            u_ref = u0_ref if g < groups_per_half else u1_ref
            gl = g % groups_per_half
            u = u_ref[rows, gl * SG_GROUP_W:(gl + 1) * SG_GROUP_W].astype(F32)
            sg_ref[rows, cs] = (u * mixed).astype(BF16)

    s_in = sg_ref[...]
    for cs, ba, ga_ref, gs_ref in zip(halves, br_a, (ga0_ref, ga1_ref), (gs0_ref, gs1_ref)):
        br_s = jnp.dot(s_in, ws_ref[:, cs], preferred_element_type=F32)
        mix = ga_ref[...].astype(F32) * ba + gs_ref[...].astype(F32) * br_s
        mix_ref[:, cs] = mix.astype(BF16)
    for cs in halves:
        x = jnp.where(pl.program_id(0) < first_tiles, xa_ref[:, cs], xb_ref[:, cs])
        o_ref[:, cs] = x + jnp.dot(mix_ref[...], wo_ref[:, cs], preferred_element_type=F32)


def _mix(att, y, lg, lb, wsp, bs_t, xa, xb, first_rows, wa, ws, wo, layer):
    t = att.shape[0]
    xa_spec, xb_spec = _split_rows(MIX_TM, first_rows)
    row = lambda w: pl.BlockSpec((MIX_TM, w), lambda i: (i, 0))
    half = lambda off: pl.BlockSpec((MIX_TM, HALF), lambda i: (i, off // HALF))
    vec = pl.BlockSpec((1, SG_WIDTH), lambda i: (0, 0))
    weight = _resident((None, D_MODEL, D_MODEL), lambda i: (layer, 0, 0))
    return pl.pallas_call(
        functools.partial(_mix_kernel, first_tiles=first_rows // MIX_TM),
        grid=(t // MIX_TM,),
        in_specs=[
            row(Q_COLS),
            half(U_OFF), half(U_OFF + HALF), half(VS_OFF), half(VS_OFF + HALF),
            vec, vec,
            pl.BlockSpec((SG_GROUPS, SG_CHUNK, SG_CHUNK), lambda i: (0, 0, 0)),
            pl.BlockSpec((SG_CHUNK, SG_GROUPS), lambda i: (0, 0)),
            half(GA_OFF), half(GA_OFF + HALF), half(GS_OFF), half(GS_OFF + HALF),
            xa_spec, xb_spec,
            weight, weight, weight,
        ],
        out_specs=row(D_MODEL),
        out_shape=jax.ShapeDtypeStruct((t, D_MODEL), F32),
        scratch_shapes=[pltpu.VMEM((MIX_TM, SG_WIDTH), BF16), pltpu.VMEM((MIX_TM, D_MODEL), BF16)],
        compiler_params=_params("parallel"),
        name="branch_mix",
    )(att, y, y, y, y, lg, lb, wsp, bs_t, y, y, y, y, xa, xb, wa, ws, wo)


def _mem_fold_kernel(k_ref, v_ref, wq_ref, wo_ref, qk_ref, vo_ref):
    n_mem = k_ref.shape[0]
    scale = X_HEAD_DIM ** -0.5
    for hd in range(X_HEADS):
        cs = slice(hd * X_HEAD_DIM, (hd + 1) * X_HEAD_DIM)
        ms = slice(hd * n_mem, (hd + 1) * n_mem)
        qk = lax.dot_general(wq_ref[:, cs], k_ref[:, cs], (((1,), (1,)), ((), ())),
                             preferred_element_type=F32)
        qk_ref[:, ms] = (qk * scale).astype(BF16)
        vo_ref[ms, :] = jnp.dot(v_ref[:, cs], wo_ref[cs, :],
                                preferred_element_type=F32).astype(BF16)


def _mem_fold(kv, wq, wo, layer, n_mem):
    n_seq = kv.shape[0] // n_mem
    weight = _resident((None, D_MODEL, D_MODEL), lambda b: (layer, 0, 0))
    return pl.pallas_call(
        _mem_fold_kernel,
        grid=(n_seq,),
        in_specs=[
            pl.BlockSpec((n_mem, D_MODEL), lambda b: (b, 0)),
            pl.BlockSpec((n_mem, D_MODEL), lambda b: (b, 1)),
            weight, weight,
        ],
        out_specs=[
            pl.BlockSpec((None, D_MODEL, X_HEADS * n_mem), lambda b: (b, 0, 0)),
            pl.BlockSpec((None, X_HEADS * n_mem, D_MODEL), lambda b: (b, 0, 0)),
        ],
        out_shape=[
            jax.ShapeDtypeStruct((n_seq, D_MODEL, X_HEADS * n_mem), BF16),
            jax.ShapeDtypeStruct((n_seq, X_HEADS * n_mem, D_MODEL), BF16),
        ],
        compiler_params=_params("parallel"),
        name="memory_fold",
    )(kv, kv, wq, wo)


def _xattn_kernel(x_ref, g_ref, qk_ref, vo_ref, o_ref, p_ref):
    n_mem = qk_ref.shape[1] // X_HEADS
    h = _rms(x_ref[...], g_ref[...]).astype(BF16)
    s = jnp.dot(h, qk_ref[...], preferred_element_type=F32)
    for hd in range(X_HEADS):
        ms = slice(hd * n_mem, (hd + 1) * n_mem)
        e = jnp.exp(s[:, ms] - jnp.max(s[:, ms], axis=-1, keepdims=True))
        p_ref[:, ms] = (e / jnp.sum(e, axis=-1, keepdims=True)).astype(BF16)
    for n in range(D_MODEL // HALF):
        cs = slice(n * HALF, (n + 1) * HALF)
        o_ref[:, cs] = x_ref[:, cs] + jnp.dot(p_ref[...], vo_ref[:, cs],
                                              preferred_element_type=F32)


def _xattn(x, g, qk, vo, seq):
    t = x.shape[0]
    tiles_per_seq = seq // XA_TM
    row = pl.BlockSpec((XA_TM, D_MODEL), lambda i: (i, 0))
    return pl.pallas_call(
        _xattn_kernel,
        grid=(t // XA_TM,),
        in_specs=[
            row,
            pl.BlockSpec((1, D_MODEL), lambda i: (0, 0)),
            pl.BlockSpec((None,) + qk.shape[1:], lambda i: (i // tiles_per_seq, 0, 0)),
            pl.BlockSpec((None,) + vo.shape[1:], lambda i: (i // tiles_per_seq, 0, 0)),
        ],
        out_specs=row,
        out_shape=jax.ShapeDtypeStruct((t, D_MODEL), F32),
        scratch_shapes=[pltpu.VMEM((XA_TM, qk.shape[2]), BF16)],
        compiler_params=_params("parallel"),
        name="memory_attention",
    )(x, g, qk, vo)


def _ffn_kernel(x_ref, xp_ref, xn_ref, g_ref, wa_ref, wb_ref, cw_ref, cb_ref, wd_ref, fg_ref,
                *refs, tiles_per_seq, split_tiles):
    if split_tiles is None:
        o_ref, h_ref = refs
        acc_ref = o_ref
    else:
        oa_ref, ob_ref, h_ref, acc_ref = refs
    i = pl.program_id(0)
    f = pl.program_id(1)
    tm = x_ref.shape[0]
    ext = tm + 2 * HALO

    @pl.when(f == 0)
    def _():
        pos = i % tiles_per_seq
        hp = _rms(xp_ref[...], g_ref[...])
        hn = _rms(xn_ref[...], g_ref[...])
        h_ref[:tm, :] = _rms(x_ref[...], g_ref[...]).astype(BF16)
        h_ref[tm:tm + HALO, :] = jnp.where(pos != tiles_per_seq - 1, hn, 0.0).astype(BF16)
        h_ref[tm + HALO:, :] = jnp.where(pos != 0, hp, 0.0).astype(BF16)
        acc_ref[...] = x_ref[...]

    a_ext = jnp.dot(h_ref[...], wa_ref[...], preferred_element_type=F32)
    b = jnp.dot(h_ref[:tm, :], wb_ref[...], preferred_element_type=F32)
    a_prev = pltpu.roll(a_ext, 1, 0)[:tm]
    a_next = pltpu.roll(a_ext, ext - 1, 0)[:tm]
    conv = (cb_ref[...] + a_prev * cw_ref[0:1, :] + a_ext[:tm] * cw_ref[1:2, :]
            + a_next * cw_ref[2:3, :])
    y = (_gelu_tanh(conv) * b).astype(BF16)
    acc_ref[...] += jnp.dot(y, wd_ref[...], preferred_element_type=F32)

    if split_tiles is not None:
        last = f == pl.num_programs(1) - 1

        @pl.when(last & (i < split_tiles))
        def _():
            oa_ref[...] = _rms(acc_ref[...], fg_ref[...])

        @pl.when(last & (i >= split_tiles))
        def _():
            ob_ref[...] = _rms(acc_ref[...], fg_ref[...])


def _ffn(x, g, w_up, conv_w, conv_b, w_down, fg, layer, seq, split_rows):
    t = x.shape[0]
    nf = D_FF // FFN_TF
    tiles_per_seq = seq // FFN_TM
    halo_blocks = FFN_TM // HALO
    last_halo = t // HALO - 1
    row = pl.BlockSpec((FFN_TM, D_MODEL), lambda i, f: (i, 0))
    vec = pl.BlockSpec((1, D_MODEL), lambda i, f: (0, 0))
    scratch = [pltpu.VMEM((FFN_TM + 2 * HALO, D_MODEL), BF16)]
    if split_rows is None:
        split_tiles = None
        out_specs = row
        out_shape = jax.ShapeDtypeStruct((t, D_MODEL), F32)
    else:
        split_tiles = split_rows // FFN_TM
        out_specs = list(_split_rows(FFN_TM, split_rows))
        out_shape = [jax.ShapeDtypeStruct((split_rows, D_MODEL), F32),
                     jax.ShapeDtypeStruct((t - split_rows, D_MODEL), F32)]
        scratch.append(pltpu.VMEM((FFN_TM, D_MODEL), F32))
    kern = functools.partial(_ffn_kernel, tiles_per_seq=tiles_per_seq, split_tiles=split_tiles)
    return pl.pallas_call(
        kern,
        grid=(t // FFN_TM, nf),
        in_specs=[
            row,
            pl.BlockSpec((HALO, D_MODEL), lambda i, f: (jnp.maximum(i * halo_blocks - 1, 0), 0)),
            pl.BlockSpec((HALO, D_MODEL),
                         lambda i, f: (jnp.minimum((i + 1) * halo_blocks, last_halo), 0)),
            vec,
            pl.BlockSpec((None, D_MODEL, FFN_TF), lambda i, f: (layer, 0, f)),
            pl.BlockSpec((None, D_MODEL, FFN_TF), lambda i, f: (layer, 0, nf + f)),
            pl.BlockSpec((3, FFN_TF), lambda i, f: (0, f)),
            pl.BlockSpec((1, FFN_TF), lambda i, f: (0, f)),
            pl.BlockSpec((None, FFN_TF, D_MODEL), lambda i, f: (layer, f, 0)),
            vec,
        ],
        out_specs=out_specs,
        out_shape=out_shape,
        scratch_shapes=scratch,
        compiler_params=_params("parallel" if split_rows is None else "arbitrary", "arbitrary"),
        name="conv_ffn",
    )(x, x, x, g, w_up, w_up, conv_w, conv_b, w_down, fg)


def _rope_tables(seq):
    t = jnp.arange(seq)
    row = (t // GRID_W).astype(F32)
    col = (t % GRID_W).astype(F32)
    quarter = HEAD_DIM // 4
    inv = ROPE_BASE ** (-jnp.arange(quarter, dtype=F32) / quarter)
    ang_r = row[:, None] * inv[None, :]
    ang_c = col[:, None] * inv[None, :]
    ang = jnp.concatenate([ang_r, ang_r, ang_c, ang_c], axis=-1)
    cos, sin = jnp.cos(ang), jnp.sin(ang)
    first = (jnp.arange(HEAD_DIM) % (2 * quarter)) < quarter
    return cos, jnp.where(first, -sin, 0.0), jnp.where(first, 0.0, sin)


def kernel(x_prompt, x_sample, mem_prompt, mem_sample, norm_mix_g, w_in, q_norm_g, k_norm_g, w_attn_o, sg_norm_g, sg_norm_b, w_spatial, b_spatial, w_sg_o, w_out, norm_x_g, norm_mem_g, w_xq, w_xkv, w_xo, norm_ffn_g, w_ffn_up, conv_w, conv_b, w_ffn_down, final_norm_g):
    depth = w_in.shape[0]
    seq = x_prompt.shape[1]
    assert x_sample.shape[1] == seq and seq % IN_TM == 0 and seq % GRID_W == 0
    assert w_in.shape[1:] == (D_MODEL, N_IN) and w_ffn_up.shape[2] == 2 * D_FF
    n_prompt = x_prompt.shape[0] * seq

    xa, xb, first_rows = x_prompt.reshape(-1, D_MODEL), x_sample.reshape(-1, D_MODEL), n_prompt
    total_rows = n_prompt + x_sample.shape[0] * seq
    mem = jnp.concatenate([mem_prompt.reshape(-1, D_MODEL), mem_sample.reshape(-1, D_MODEL)], axis=0)
    n_mem = mem_prompt.shape[1]
    assert mem_sample.shape[1] == n_mem

    cos, sina, sinb = _rope_tables(seq)
    row = lambda v: v.reshape(1, -1).astype(F32)
    bf = lambda w: w.astype(BF16)
    w_in_b, w_attn_o_b, w_sg_o_b, w_out_b = bf(w_in), bf(w_attn_o), bf(w_sg_o), bf(w_out)
    w_xq_b, w_xkv_b, w_xo_b = bf(w_xq), bf(w_xkv), bf(w_xo)
    w_up_b, w_down_b, w_sp_b = bf(w_ffn_up), bf(w_ffn_down), bf(w_spatial)
    q_scale = HEAD_DIM ** -0.5 * LOG2_E
    fg = row(final_norm_g)
    mem_tm = MEM_TM_MAX
    while mem.shape[0] % mem_tm:
        mem_tm //= 2
    assert mem_tm % SUBLANES == 0

    for l in range(depth):
        y = _in_proj(xa, None if xb is xa else xb, first_rows, row(norm_mix_g[l]), w_in_b, l,
                     cos, sina, sinb, row(q_norm_g[l]) * q_scale, row(k_norm_g[l]), seq)
        att = _attention(y, seq)
        x = _mix(att, y, row(sg_norm_g[l]), row(sg_norm_b[l]), w_sp_b[l], b_spatial[l].T,
                 xa, xb, first_rows, w_attn_o_b, w_sg_o_b, w_out_b, l)
        kv = _rms_matmul(mem, row(norm_mem_g[l]), w_xkv_b, l, mem_tm, MEM_TN)
        qk, vo = _mem_fold(kv, w_xq_b, w_xo_b, l, n_mem)
        x = _xattn(x, row(norm_x_g[l]), qk, vo, seq)
        x = _ffn(x, row(norm_ffn_g[l]), w_up_b, conv_w[l], row(conv_b[l]), w_down_b, fg,
                 l, seq, n_prompt if l == depth - 1 else None)
        xa, xb, first_rows = x, x, total_rows

    y_prompt, y_sample = x
    return (y_prompt.reshape(x_prompt.shape), y_sample.reshape(x_sample.shape))
```
